```python
import math
import jax, jax.numpy as jnp
from jax import lax
import numpy as np

D_MODEL = 1024
BATCH = 4
SEQ = 8192
DEPTH = 1

CHUNK = 64
D_MIX = 2 * D_MODEL
D_MLSTM = D_MIX // 2
D_RGLRU = D_MIX - D_MLSTM
MLSTM_HEADS = 4
MLSTM_HEAD_DIM = D_MLSTM // MLSTM_HEADS
QKV_BLOCK = 4
RGLRU_BLOCKS = 4
RGLRU_BLOCK_W = D_RGLRU // RGLRU_BLOCKS
CONV_W = 4
RGLRU_C = 8.0
D_FF = 2816
EPS = 1e-6
M_INIT = -1e30

kernel_name = "hymba_mlstm_rglru_macaron_block"


def rmsnorm(x, g):
    xf = x.astype(jnp.float32)
    y = xf * lax.rsqrt(jnp.mean(xf * xf, axis=-1, keepdims=True) + EPS) * g.astype(jnp.float32)
    return y.astype(x.dtype)


def swiglu(x, wg, wu, wd):
    return (jax.nn.silu(x @ wg) * (x @ wu)) @ wd


def causal_dwconv(x, w, b):
    s = x.shape[1]
    xp = jnp.pad(x, ((0, 0), (CONV_W - 1, 0), (0, 0)))
    y = b
    for tap in range(CONV_W):
        y = y + w[tap] * xp[:, tap:tap + s]
    return y


def blockdiag(x, w):
    bsz, s, _ = x.shape
    nb, bi, bo = w.shape
    y = jnp.einsum('bsni,nio->bsno', x.reshape(bsz, s, nb, bi), w)
    return y.reshape(bsz, s, nb * bo)


def _mlstm_chunk_step(carry, inp):
    c_mem, n_mem, m_prev = carry
    q, k, v, ig, lf = inp
    L = q.shape[2]
    b = jnp.cumsum(lf, axis=-1)
    g = b[..., -1]
    causal = jnp.tril(jnp.ones((L, L), dtype=bool))
    dlog = b[..., :, None] - b[..., None, :] + ig[..., None, :]
    dlog = jnp.where(causal, dlog, -jnp.inf)
    inter = b + m_prev[..., None]
    m_row = jnp.maximum(inter, jnp.max(dlog, axis=-1))
    s = jnp.einsum('bhid,bhjd->bhij', q, k) * jnp.exp(dlog - m_row[..., None])
    inter_w = jnp.exp(inter - m_row)
    num = (jnp.einsum('bhij,bhjd->bhid', s, v)
           + inter_w[..., None] * jnp.einsum('bhvk,bhik->bhiv', c_mem, q))
    den = jnp.sum(s, axis=-1) + inter_w * jnp.einsum('bhk,bhik->bhi', n_mem, q)
    h = num / jnp.maximum(jnp.abs(den), jnp.exp(-m_row))[..., None]
    w = g[..., None] - b + ig
    m_new = jnp.maximum(g + m_prev, jnp.max(w, axis=-1))
    decay = jnp.exp(g + m_prev - m_new)
    wk = jnp.exp(w - m_new[..., None])
    c_new = decay[..., None, None] * c_mem + jnp.einsum('bhjv,bhjk->bhvk', v * wk[..., None], k)
    n_new = decay[..., None] * n_mem + jnp.einsum('bhj,bhjk->bhk', wk, k)
    return (c_new, n_new, m_new), h


def mlstm_group(x_m, z_m, conv_w, conv_b, wq, wk, wv, w_gates, b_gates, ln_w, skip):
    bsz, s, _ = x_m.shape
    H, dh = MLSTM_HEADS, MLSTM_HEAD_DIM
    nc = s // CHUNK
    xc = jax.nn.silu(causal_dwconv(x_m, conv_w, conv_b))
    q = blockdiag(xc, wq)
    k = blockdiag(xc, wk)
    v = blockdiag(x_m, wv)
    gates = (jnp.concatenate([q, k, v], axis=-1) @ w_gates + b_gates).astype(jnp.float32)
    ig = gates[..., :H]
    lf = jax.nn.log_sigmoid(gates[..., H:])

    def to_chunks(t):
        return t.astype(jnp.float32).reshape(bsz, nc, CHUNK, H, dh).transpose(1, 0, 3, 2, 4)

    def gate_chunks(t):
        return t.reshape(bsz, nc, CHUNK, H).transpose(1, 0, 3, 2)

    xs = (to_chunks(q), to_chunks(k) * (dh ** -0.5), to_chunks(v), gate_chunks(ig), gate_chunks(lf))
    carry0 = (jnp.zeros((bsz, H, dh, dh), jnp.float32),
              jnp.zeros((bsz, H, dh), jnp.float32),
              jnp.full((bsz, H), M_INIT, jnp.float32))
    _, hs = lax.scan(_mlstm_chunk_step, carry0, xs)
    h = hs.transpose(1, 0, 3, 2, 4).reshape(bsz, s, H, dh)
    mu = jnp.mean(h, axis=-1, keepdims=True)
    var = jnp.mean(jnp.square(h - mu), axis=-1, keepdims=True)
    hn = (h - mu) * lax.rsqrt(var + EPS) * ln_w.astype(jnp.float32).reshape(H, dh)
    hn = hn.reshape(bsz, s, D_MLSTM).astype(x_m.dtype)
    return (hn + skip * xc) * jax.nn.silu(z_m)


def _lin_combine(left, right):
    a_l, b_l = left
    a_r, b_r = right
    return a_l * a_r, a_r * b_l + b_r


def rglru_group(x_r, y_r, conv_w, conv_b, w_a, b_a, w_x, b_x, lam):
    xc = causal_dwconv(x_r, conv_w, conv_b)
    r = jax.nn.sigmoid((blockdiag(xc, w_a) + b_a).astype(jnp.float32))
    i = jax.nn.sigmoid((blockdiag(xc, w_x) + b_x).astype(jnp.float32))
    log_a = RGLRU_C * r * jax.nn.log_sigmoid(lam.astype(jnp.float32))
    a = jnp.exp(log_a)
    u = jnp.sqrt(-jnp.expm1(2.0 * log_a)) * (i * xc.astype(jnp.float32))
    _, h = lax.associative_scan(_lin_combine, (a, u), axis=1)
    return h.astype(x_r.dtype) * jax.nn.gelu(y_r)


def setup_inputs(seed: int = 0) -> dict:
    key = jax.random.key(seed)
    ks = iter(jax.random.split(key, 48))
    Ly = DEPTH
    H = MLSTM_HEADS

    def nrm(shape, scale):
        return jax.random.normal(next(ks), shape, jnp.float32) * scale

    def gain(shape):
        return 1.0 + nrm(shape, 0.02)

    x = nrm((BATCH, SEQ, D_MODEL), 1.0)
    norm_ffn1 = gain((Ly, D_MODEL))
    ffn1_wg = nrm((Ly, D_MODEL, D_FF), D_MODEL ** -0.5)
    ffn1_wu = nrm((Ly, D_MODEL, D_FF), D_MODEL ** -0.5)
    ffn1_wd = nrm((Ly, D_FF, D_MODEL), D_FF ** -0.5)
    norm_mix = gain((Ly, D_MODEL))
    w_in = nrm((Ly, D_MODEL, 2 * D_MLSTM + 2 * D_RGLRU), D_MODEL ** -0.5)
    m_conv_w = nrm((Ly, CONV_W, D_MLSTM), CONV_W ** -0.5)
    m_conv_b = nrm((Ly, D_MLSTM), 0.02)
    nqb = D_MLSTM // QKV_BLOCK
    m_wq = nrm((Ly, nqb, QKV_BLOCK, QKV_BLOCK), QKV_BLOCK ** -0.5)
    m_wk = nrm((Ly, nqb, QKV_BLOCK, QKV_BLOCK), QKV_BLOCK ** -0.5)
    m_wv = nrm((Ly, nqb, QKV_BLOCK, QKV_BLOCK), QKV_BLOCK ** -0.5)
    m_w_gates = nrm((Ly, 3 * D_MLSTM, 2 * H), (3 * D_MLSTM) ** -0.5)
    i_bias = nrm((Ly, H), 0.1)
    f_bias = jnp.linspace(3.0, 6.0, H, dtype=jnp.float32) + nrm((Ly, H), 0.1)
    m_b_gates = jnp.concatenate([i_bias, f_bias], axis=-1)
    m_ln_w = gain((Ly, D_MLSTM))
    m_skip = gain((Ly, D_MLSTM))
    r_conv_w = nrm((Ly, CONV_W, D_RGLRU), CONV_W ** -0.5)
    r_conv_b = nrm((Ly, D_RGLRU), 0.02)
    r_w_a = nrm((Ly, RGLRU_BLOCKS, RGLRU_BLOCK_W, RGLRU_BLOCK_W), RGLRU_BLOCK_W ** -0.5)
    r_b_a = nrm((Ly, D_RGLRU), 0.02)
    r_w_x = nrm((Ly, RGLRU_BLOCKS, RGLRU_BLOCK_W, RGLRU_BLOCK_W), RGLRU_BLOCK_W ** -0.5)
    r_b_x = nrm((Ly, D_RGLRU), 0.02)
    a0 = jax.random.uniform(next(ks), (Ly, D_RGLRU), jnp.float32, minval=0.9, maxval=0.999)
    sa = a0 ** (1.0 / RGLRU_C)
    r_lam = jnp.log(sa) - jnp.log1p(-sa)
    out_norm_m = gain((Ly, D_MLSTM))
    out_norm_r = gain((Ly, D_RGLRU))
    w_out = nrm((Ly, D_MIX, D_MODEL), D_MIX ** -0.5)
    norm_ffn2 = gain((Ly, D_MODEL))
    ffn2_wg = nrm((Ly, D_MODEL, D_FF), D_MODEL ** -0.5)
    ffn2_wu = nrm((Ly, D_MODEL, D_FF), D_MODEL ** -0.5)
    ffn2_wd = nrm((Ly, D_FF, D_MODEL), D_FF ** -0.5)
    norm_final = gain((D_MODEL,))
    return {"x": x, "norm_ffn1": norm_ffn1, "ffn1_wg": ffn1_wg, "ffn1_wu": ffn1_wu,
            "ffn1_wd": ffn1_wd, "norm_mix": norm_mix, "w_in": w_in,
            "m_conv_w": m_conv_w, "m_conv_b": m_conv_b, "m_wq": m_wq, "m_wk": m_wk,
            "m_wv": m_wv, "m_w_gates": m_w_gates, "m_b_gates": m_b_gates,
            "m_ln_w": m_ln_w, "m_skip": m_skip, "r_conv_w": r_conv_w,
            "r_conv_b": r_conv_b, "r_w_a": r_w_a, "r_b_a": r_b_a, "r_w_x": r_w_x,
            "r_b_x": r_b_x, "r_lam": r_lam, "out_norm_m": out_norm_m,
            "out_norm_r": out_norm_r, "w_out": w_out, "norm_ffn2": norm_ffn2,
            "ffn2_wg": ffn2_wg, "ffn2_wu": ffn2_wu, "ffn2_wd": ffn2_wd,
            "norm_final": norm_final}


def reference(x, norm_ffn1, ffn1_wg, ffn1_wu, ffn1_wd, norm_mix, w_in, m_conv_w, m_conv_b,
              m_wq, m_wk, m_wv, m_w_gates, m_b_gates, m_ln_w, m_skip, r_conv_w, r_conv_b,
              r_w_a, r_b_a, r_w_x, r_b_x, r_lam, out_norm_m, out_norm_r, w_out, norm_ffn2,
              ffn2_wg, ffn2_wu, ffn2_wd, norm_final):
    split_at = [D_MLSTM, 2 * D_MLSTM, 2 * D_MLSTM + D_RGLRU]
    for l in range(DEPTH):
        x = x + 0.5 * swiglu(rmsnorm(x, norm_ffn1[l]), ffn1_wg[l], ffn1_wu[l], ffn1_wd[l])
        h = rmsnorm(x, norm_mix[l])
        proj = h @ w_in[l]
        x_m, z_m, x_r, y_r = jnp.split(proj, split_at, axis=-1)
        out_m = mlstm_group(x_m, z_m, m_conv_w[l], m_conv_b[l], m_wq[l], m_wk[l], m_wv[l],
                            m_w_gates[l], m_b_gates[l], m_ln_w[l], m_skip[l]).astype(x.dtype)
        out_r = rglru_group(x_r, y_r, r_conv_w[l], r_conv_b[l], r_w_a[l], r_b_a[l],
                            r_w_x[l], r_b_x[l], r_lam[l]).astype(x.dtype)
        mixed = jnp.concatenate([rmsnorm(out_m, out_norm_m[l]),
                                 rmsnorm(out_r, out_norm_r[l])], axis=-1)
        x = x + mixed @ w_out[l]
        x = x + 0.5 * swiglu(rmsnorm(x, norm_ffn2[l]), ffn2_wg[l], ffn2_wu[l], ffn2_wd[l])
    return rmsnorm(x, norm_final)
```

```python
import functools

import jax
import jax.numpy as jnp
from jax import lax
from jax.experimental import pallas as pl
from jax.experimental.pallas import tpu as pltpu

D_MODEL = 1024
D_FF = 2816
D_MLSTM = 1024
D_RGLRU = 1024
HEADS = 4
HEAD_DIM = 256
QKV_BLOCK = 4
CONV_W = 4
RGLRU_C = 8.0
EPS = 1e-6
M_INIT = -1e30
N_GATES = 2 * HEADS

GATE_LANES = 128
TAIL_ROWS = 8
FFN_TILE = 256
MIX_TILE = 256
VMEM_LIMIT = 56 * 1024 * 1024

F32 = jnp.float32
BF16 = jnp.bfloat16


def _rms(x, g):
    return x * lax.rsqrt(jnp.mean(x * x, axis=-1, keepdims=True) + EPS) * g


def _dot(a, b):
    return jnp.dot(a, b, preferred_element_type=F32)


def _swiglu_residual(x, g_ref, wg_ref, wu_ref, wd_ref):
    xn = _rms(x, g_ref[...]).astype(BF16)
    gate = _dot(xn, wg_ref[...])
    up = _dot(xn, wu_ref[...])
    h = (jax.nn.silu(gate) * up).astype(BF16)
    return x + 0.5 * _dot(h, wd_ref[...])


def _ffn_in_kernel(x_ref, g1_ref, wg_ref, wu_ref, wd_ref, g2_ref, win_ref, x1_ref, proj_ref):
    y = _swiglu_residual(x_ref[...], g1_ref, wg_ref, wu_ref, wd_ref)
    x1_ref[...] = y
    proj_ref[...] = _dot(_rms(y, g2_ref[...]).astype(BF16), win_ref[...])


def _ffn_out_kernel(x_ref, g1_ref, wg_ref, wu_ref, wd_ref, g2_ref, o_ref):
    y = _swiglu_residual(x_ref[...], g1_ref, wg_ref, wu_ref, wd_ref)
    o_ref[...] = _rms(y, g2_ref[...])


def _causal_conv(ext_ref, x, w_ref, b_ref):
    t = x.shape[0]
    ext_ref[TAIL_ROWS:TAIL_ROWS + t, :] = x
    y = b_ref[...]
    for tap in range(CONV_W - 1):
        lo = TAIL_ROWS - (CONV_W - 1) + tap
        y = y + w_ref[tap:tap + 1, :] * ext_ref[lo:lo + t, :]
    y = y + w_ref[CONV_W - 1:CONV_W, :] * x
    ext_ref[0:TAIL_ROWS, :] = ext_ref[t:t + TAIL_ROWS, :]
    return y


def _mlstm_head(h, q, k, v, xc_h, z_h, gcol, grow, bcol, brow, causal,
                ct_ref, n_ref, m_ref, lnw_ref, skip_ref):
    t = q.shape[0]
    qb = q.astype(BF16)
    ks = k * (HEAD_DIM ** -0.5)
    kb = ks.astype(BF16)
    b_i = bcol[:, HEADS + h:HEADS + h + 1]
    ig_i = gcol[:, h:h + 1]
    b_j = brow[HEADS + h:HEADS + h + 1, :]
    ig_j = grow[h:h + 1, :]
    m_prev = m_ref[h:h + 1, 0:1]

    dlog = jnp.where(causal, b_i - b_j + ig_j, -jnp.inf)
    inter = b_i + m_prev
    m_row = jnp.maximum(inter, jnp.max(dlog, axis=-1, keepdims=True))
    s = lax.dot_general(qb, kb, (((1,), (1,)), ((), ())), preferred_element_type=F32)
    s = s * jnp.exp(dlog - m_row)
    inter_w = jnp.exp(inter - m_row)
    num = _dot(s.astype(BF16), v.astype(BF16)) + inter_w * _dot(qb, ct_ref[h].astype(BF16))
    den = (jnp.sum(s, axis=-1, keepdims=True)
           + inter_w * jnp.sum(q * n_ref[h:h + 1, :], axis=-1, keepdims=True))
    hh = num / jnp.maximum(jnp.abs(den), jnp.exp(-m_row))

    g = b_i[t - 1:t, :]
    w = g - b_i + ig_i
    m_new = jnp.maximum(g + m_prev, jnp.max(w, axis=0, keepdims=True))
    decay = jnp.exp(g + m_prev - m_new)
    wk = jnp.exp(w - m_new)
    ct_ref[h] = decay * ct_ref[h] + lax.dot_general(
        kb, (v * wk).astype(BF16), (((0,), (0,)), ((), ())), preferred_element_type=F32)
    n_ref[h:h + 1, :] = decay * n_ref[h:h + 1, :] + jnp.sum(ks * wk, axis=0, keepdims=True)
    m_ref[h:h + 1, :] = jnp.broadcast_to(m_new, (1, m_ref.shape[1]))

    sl = slice(h * HEAD_DIM, (h + 1) * HEAD_DIM)
    mu = jnp.mean(hh, axis=-1, keepdims=True)
    var = jnp.mean(jnp.square(hh - mu), axis=-1, keepdims=True)
    hn = (hh - mu) * lax.rsqrt(var + EPS) * lnw_ref[:, sl]
    return (hn + skip_ref[:, sl] * xc_h) * jax.nn.silu(z_h)


def _mix_kernel(xm_ref, zm_ref, xr_ref, yr_ref, x1_ref,
                mcw_ref, mcb_ref, wqk_ref, wv_ref, wgc_ref, wgr_ref, bgc_ref, bgr_ref,
                lnw_ref, skip_ref,
                rcw_ref, rcb_ref, wax_ref, ba_ref, bx_ref, lam_ref,
                onm_ref, onr_ref, wout_ref,
                o_ref,
                xm_ext, xr_ext, ct_ref, n_ref, m_ref, hr_ref):
    t = xm_ref.shape[0]

    @pl.when(pl.program_id(1) == 0)
    def _reset_state():
        xm_ext[0:TAIL_ROWS, :] = jnp.zeros((TAIL_ROWS, D_MLSTM), F32)
        xr_ext[0:TAIL_ROWS, :] = jnp.zeros((TAIL_ROWS, D_RGLRU), F32)
        ct_ref[...] = jnp.zeros_like(ct_ref)
        n_ref[...] = jnp.zeros_like(n_ref)
        m_ref[...] = jnp.full_like(m_ref, M_INIT)
        hr_ref[...] = jnp.zeros_like(hr_ref)

    xm = xm_ref[...]
    xc = jax.nn.silu(_causal_conv(xm_ext, xm, mcw_ref, mcb_ref))
    xc_b = xc.astype(BF16)
    xm_b = xm.astype(BF16)
    gcol = jnp.broadcast_to(bgc_ref[...], (t, GATE_LANES))
    grow = jnp.broadcast_to(bgr_ref[...], (N_GATES, t))
    qs, ks, vs = [], [], []
    nt_dims = (((1,), (1,)), ((), ()))
    for h in range(HEADS):
        sl = slice(h * HEAD_DIM, (h + 1) * HEAD_DIM)
        qk = _dot(xc_b[:, sl], wqk_ref[h])
        q, k = qk[:, :HEAD_DIM], qk[:, HEAD_DIM:]
        v = _dot(xm_b[:, sl], wv_ref[h])
        for part, val in enumerate((q, k, v)):
            vb = val.astype(BF16)
            gcol = gcol + _dot(vb, wgc_ref[part, sl, :])
            grow = grow + lax.dot_general(wgr_ref[part, :, sl], vb, nt_dims,
                                          preferred_element_type=F32)
        qs.append(q)
        ks.append(k)
        vs.append(v)

    row = lax.broadcasted_iota(jnp.int32, (t, t), 0)
    col = lax.broadcasted_iota(jnp.int32, (t, t), 1)
    causal = col <= row
    tri = causal.astype(F32)
    tri_t = (row <= col).astype(F32)
    bcol = jnp.dot(tri, jax.nn.log_sigmoid(gcol), precision=lax.Precision.HIGHEST,
                   preferred_element_type=F32)
    brow = jnp.dot(jax.nn.log_sigmoid(grow), tri_t, precision=lax.Precision.HIGHEST,
                   preferred_element_type=F32)

    zm = zm_ref[...]
    outs = []
    for h in range(HEADS):
        sl = slice(h * HEAD_DIM, (h + 1) * HEAD_DIM)
        outs.append(_mlstm_head(h, qs[h], ks[h], vs[h], xc[:, sl], zm[:, sl],
                                gcol, grow, bcol, brow, causal,
                                ct_ref, n_ref, m_ref, lnw_ref, skip_ref))
    out_m = jnp.concatenate(outs, axis=1)

    xr = xr_ref[...]
    xrc = _causal_conv(xr_ext, xr, rcw_ref, rcb_ref)
    xrc_b = xrc.astype(BF16)
    pre = [_dot(xrc_b[:, g * HEAD_DIM:(g + 1) * HEAD_DIM], wax_ref[g]) for g in range(HEADS)]
    pre_a = jnp.concatenate([p[:, :HEAD_DIM] for p in pre], axis=1) + ba_ref[...]
    pre_x = jnp.concatenate([p[:, HEAD_DIM:] for p in pre], axis=1) + bx_ref[...]
    r = jax.nn.sigmoid(pre_a)
    i = jax.nn.sigmoid(pre_x)
    log_a = RGLRU_C * r * jax.nn.log_sigmoid(lam_ref[...])
    a = jnp.exp(log_a)
    u = jnp.sqrt(-jnp.tanh(log_a) * (1.0 + a * a)) * (i * xrc)
    rowi = lax.broadcasted_iota(jnp.int32, (t, 1), 0)
    d = 1
    while d < t:
        a_sh = jnp.where(rowi < d, 1.0, pltpu.roll(a, d, 0))
        u_sh = jnp.where(rowi < d, 0.0, pltpu.roll(u, d, 0))
        u = a * u_sh + u
        a = a * a_sh
        d *= 2
    hseq = u + a * hr_ref[0:1, :]
    hr_ref[0:1, :] = hseq[t - 1:t, :]
    out_r = hseq * jax.nn.gelu(yr_ref[...])

    nm = _rms(out_m, onm_ref[...]).astype(BF16)
    nr = _rms(out_r, onr_ref[...]).astype(BF16)
    o_ref[...] = (x1_ref[...] + _dot(nm, wout_ref[0:D_MLSTM, :])
                  + _dot(nr, wout_ref[D_MLSTM:D_MLSTM + D_RGLRU, :]))


def _resident(shape):
    nd = len(shape)
    return pl.BlockSpec(shape, lambda *_: (0,) * nd, pipeline_mode=pl.Buffered(1))


def _blockdiag_dense(w):
    per = HEAD_DIM // QKV_BLOCK
    w = w.reshape(HEADS, per, QKV_BLOCK, QKV_BLOCK)
    eye = jnp.eye(per, dtype=w.dtype)
    return jnp.einsum('hnio,nm->hnimo', w, eye).reshape(HEADS, HEAD_DIM, HEAD_DIM)


def _row(v):
    return v.reshape(1, -1).astype(F32)


def _ffn_specs(tile):
    return [
        pl.BlockSpec((tile, D_MODEL), lambda i: (i, 0)),
        _resident((1, D_MODEL)),
        _resident((D_MODEL, D_FF)),
        _resident((D_MODEL, D_FF)),
        _resident((D_FF, D_MODEL)),
        _resident((1, D_MODEL)),
    ]


def kernel(x, norm_ffn1, ffn1_wg, ffn1_wu, ffn1_wd, norm_mix, w_in, m_conv_w, m_conv_b, m_wq, m_wk, m_wv, m_w_gates, m_b_gates, m_ln_w, m_skip, r_conv_w, r_conv_b, r_w_a, r_b_a, r_w_x, r_b_x, r_lam, out_norm_m, out_norm_r, w_out, norm_ffn2, ffn2_wg, ffn2_wu, ffn2_wd, norm_final):
    bsz, seq, _ = x.shape
    n_tok = bsz * seq
    assert norm_ffn1.shape[0] == 1, "single-layer block"
    assert seq % MIX_TILE == 0 and n_tok % FFN_TILE == 0
    d_proj = w_in.shape[-1]
    xf = x.reshape(n_tok, D_MODEL)
    params = pltpu.CompilerParams(dimension_semantics=("arbitrary",), vmem_limit_bytes=VMEM_LIMIT)

    x1, proj = pl.pallas_call(
        _ffn_in_kernel,
        grid=(n_tok // FFN_TILE,),
        in_specs=_ffn_specs(FFN_TILE) + [_resident((D_MODEL, d_proj))],
        out_specs=[pl.BlockSpec((FFN_TILE, D_MODEL), lambda i: (i, 0)),
                   pl.BlockSpec((FFN_TILE, d_proj), lambda i: (i, 0))],
        out_shape=[jax.ShapeDtypeStruct((n_tok, D_MODEL), F32),
                   jax.ShapeDtypeStruct((n_tok, d_proj), F32)],
        compiler_params=params,
        name="ffn1_inproj",
    )(xf, _row(norm_ffn1[0]), ffn1_wg[0].astype(BF16), ffn1_wu[0].astype(BF16),
      ffn1_wd[0].astype(BF16), _row(norm_mix[0]), w_in[0].astype(BF16))

    wqk = jnp.concatenate([_blockdiag_dense(m_wq[0]), _blockdiag_dense(m_wk[0])], axis=-1).astype(BF16)
    wv = _blockdiag_dense(m_wv[0]).astype(BF16)
    wg3 = m_w_gates[0].reshape(3, D_MLSTM, N_GATES)
    wgc = jnp.pad(wg3, ((0, 0), (0, 0), (0, GATE_LANES - N_GATES))).astype(BF16)
    wgr = wg3.transpose(0, 2, 1).astype(BF16)
    bgc = jnp.pad(m_b_gates[0].astype(F32), (0, GATE_LANES - N_GATES)).reshape(1, GATE_LANES)
    bgr = m_b_gates[0].astype(F32).reshape(N_GATES, 1)
    wax = jnp.concatenate([r_w_a[0], r_w_x[0]], axis=-1).astype(BF16)

    nt = seq // MIX_TILE
    tile_spec = lambda j: pl.BlockSpec((MIX_TILE, D_MODEL), lambda b, t, j=j: (b * nt + t, j))
    weights = [
        m_conv_w[0].astype(F32), _row(m_conv_b[0]), wqk, wv, wgc, wgr, bgc, bgr,
        _row(m_ln_w[0]), _row(m_skip[0]),
        r_conv_w[0].astype(F32), _row(r_conv_b[0]), wax, _row(r_b_a[0]), _row(r_b_x[0]), _row(r_lam[0]),
        _row(out_norm_m[0]), _row(out_norm_r[0]), w_out[0].astype(BF16),
    ]
    x2 = pl.pallas_call(
        _mix_kernel,
        grid=(bsz, nt),
        in_specs=[tile_spec(0), tile_spec(1), tile_spec(2), tile_spec(3), tile_spec(0)]
                 + [_resident(w.shape) for w in weights],
        out_specs=pl.BlockSpec((MIX_TILE, D_MODEL), lambda b, t: (b * nt + t, 0)),
        out_shape=jax.ShapeDtypeStruct((n_tok, D_MODEL), F32),
        scratch_shapes=[
            pltpu.VMEM((MIX_TILE + TAIL_ROWS, D_MLSTM), F32),
            pltpu.VMEM((MIX_TILE + TAIL_ROWS, D_RGLRU), F32),
            pltpu.VMEM((HEADS, HEAD_DIM, HEAD_DIM), F32),
            pltpu.VMEM((HEADS, HEAD_DIM), F32),
            pltpu.VMEM((HEADS, GATE_LANES), F32),
            pltpu.VMEM((TAIL_ROWS, D_RGLRU), F32),
        ],
        compiler_params=pltpu.CompilerParams(dimension_semantics=("arbitrary", "arbitrary"),
                                             vmem_limit_bytes=VMEM_LIMIT),
        name="mixers_outproj",
    )(proj, proj, proj, proj, x1, *weights)

    out = pl.pallas_call(
        _ffn_out_kernel,
        grid=(n_tok // FFN_TILE,),
        in_specs=_ffn_specs(FFN_TILE),
        out_specs=pl.BlockSpec((FFN_TILE, D_MODEL), lambda i: (i, 0)),
        out_shape=jax.ShapeDtypeStruct((n_tok, D_MODEL), F32),
        compiler_params=params,
        name="ffn2_final",
    )(x2, _row(norm_ffn2[0]), ffn2_wg[0].astype(BF16), ffn2_wu[0].astype(BF16),
      ffn2_wd[0].astype(BF16), _row(norm_final))
    return out.reshape(bsz, seq, D_MODEL)
```

```python
import functools

import jax
import jax.numpy as jnp
from jax import lax
from jax.experimental import pallas as pl
from jax.experimental.pallas import tpu as pltpu

D_MODEL = 1024
D_FF = 2816
D_MLSTM = 1024
D_RGLRU = 1024
HEADS = 4
HEAD_DIM = 256
QKV_BLOCK = 4
CONV_W = 4
RGLRU_C = 8.0
EPS = 1e-6
M_INIT = -1e30
N_GATES = 2 * HEADS

SUBLANES = 8
GATE_LANES = 128
TILE = 256
SEG = TILE // SUBLANES
WRAP_ROWS = (CONV_W - 1) * SUBLANES
FF_CHUNK = 256
VMEM_LIMIT = 56 * 1024 * 1024

F32 = jnp.float32
BF16 = jnp.bfloat16


def _time_permute(x):
    t, d = x.shape
    return x.reshape(SUBLANES, t // SUBLANES, d).swapaxes(0, 1).reshape(t, d)


def _time_unpermute(x):
    t, d = x.shape
    return x.reshape(t // SUBLANES, SUBLANES, d).swapaxes(0, 1).reshape(t, d)


def _rms(x, g):
    return x * lax.rsqrt(jnp.mean(x * x, axis=-1, keepdims=True) + EPS) * g


def _dot(a, b):
    return jnp.dot(a, b, preferred_element_type=F32)


def _swiglu_residual(x, g_ref, wg_ref, wu_ref, wd_ref):
    xn = _rms(x, g_ref[...]).astype(BF16)
    gate = _dot(xn, wg_ref[...])
    up = _dot(xn, wu_ref[...])
    h = (jax.nn.silu(gate) * up).astype(BF16)
    return x + 0.5 * _dot(h, wd_ref[...])


class _ChunkedFFN:
    def __init__(self, x, g_ref, wg_ref, wu_ref, wd_ref):
        self.x = x
        self.xn = _rms(x, g_ref[...]).astype(BF16)
        self.refs = (wg_ref, wu_ref, wd_ref)
        self.acc = None
        self.done = 0

    def emit(self, n_chunks):
        wg_ref, wu_ref, wd_ref = self.refs
        for _ in range(n_chunks):
            if self.done == D_FF // FF_CHUNK:
                return
            sl = slice(self.done * FF_CHUNK, (self.done + 1) * FF_CHUNK)
            gate = _dot(self.xn, wg_ref[:, sl])
            up = _dot(self.xn, wu_ref[:, sl])
            part = _dot((jax.nn.silu(gate) * up).astype(BF16), wd_ref[sl, :])
            self.acc = part if self.acc is None else self.acc + part
            self.done += 1

    def result(self):
        self.emit(D_FF // FF_CHUNK)
        return self.x + 0.5 * self.acc


def _ffn_in_kernel(x_ref, g1_ref, wg_ref, wu_ref, wd_ref, g2_ref, win_ref, x1_ref, proj_ref):
    y = _swiglu_residual(_time_permute(x_ref[...]), g1_ref, wg_ref, wu_ref, wd_ref)
    x1_ref[...] = y
    proj_ref[...] = _dot(_rms(y, g2_ref[...]).astype(BF16), win_ref[...])


def _causal_conv(tail_ref, x, w_ref, b_ref):
    t = x.shape[0]
    sub = lax.broadcasted_iota(jnp.int32, (SUBLANES, 1), 0)
    wrap = []
    for j in range(CONV_W - 1):
        cur = x[t - WRAP_ROWS + j * SUBLANES:t - WRAP_ROWS + (j + 1) * SUBLANES, :]
        prev = tail_ref[j * SUBLANES:(j + 1) * SUBLANES, :]
        wrap.append(pltpu.roll(jnp.where(sub == SUBLANES - 1, prev, cur), 1, 0))
    tail_ref[...] = x[t - WRAP_ROWS:t, :]
    ext = jnp.concatenate(wrap + [x], axis=0)
    y = b_ref[...]
    for tap in range(CONV_W - 1):
        lo = WRAP_ROWS - (CONV_W - 1 - tap) * SUBLANES
        y = y + w_ref[tap:tap + 1, :] * ext[lo:lo + t, :]
    return y + w_ref[CONV_W - 1:CONV_W, :] * x


def _linear_scan(a, u, state_ref):
    t = a.shape[0]
    hs, ps = [], []
    h = p = None
    for r in range(t // SUBLANES):
        a_r = a[r * SUBLANES:(r + 1) * SUBLANES, :]
        u_r = u[r * SUBLANES:(r + 1) * SUBLANES, :]
        h = u_r if h is None else a_r * h + u_r
        p = a_r if p is None else a_r * p
        hs.append(h)
        ps.append(p)
    sub = lax.broadcasted_iota(jnp.int32, (SUBLANES, 1), 0)
    a_seg, u_seg = p, h
    d = 1
    while d < SUBLANES:
        a_sh = jnp.where(sub < d, 1.0, pltpu.roll(a_seg, d, 0))
        u_sh = jnp.where(sub < d, 0.0, pltpu.roll(u_seg, d, 0))
        u_seg = a_seg * u_sh + u_seg
        a_seg = a_seg * a_sh
        d *= 2
    carry_in = state_ref[0:1, :]
    seg_end = u_seg + a_seg * carry_in
    state_ref[0:1, :] = seg_end[SUBLANES - 1:SUBLANES, :]
    seg_start = jnp.where(sub == 0, carry_in, pltpu.roll(seg_end, 1, 0))
    return jnp.concatenate([h_r + p_r * seg_start for h_r, p_r in zip(hs, ps)], axis=0)


def _mlstm_head(h, q, k, v, xc_h, z_h, gcol, bcol, grow, causal,
                ct_ref, n_ref, m_ref, lnw_ref, skip_ref):
    t = q.shape[0]
    qb = q.astype(BF16)
    ks = k * (HEAD_DIM ** -0.5)
    kb = ks.astype(BF16)
    b_i = bcol[:, HEADS + h:HEADS + h + 1]
    ig_i = gcol[:, h:h + 1]
    b_j = grow[HEADS + h:HEADS + h + 1, :]
    ig_j = grow[h:h + 1, :]
    m_prev = m_ref[h:h + 1, 0:1]

    dlog = jnp.where(causal, b_i - b_j + ig_j, -jnp.inf)
    inter = b_i + m_prev
    m_row = jnp.maximum(inter, jnp.max(dlog, axis=-1, keepdims=True))
    s = lax.dot_general(qb, kb, (((1,), (1,)), ((), ())), preferred_element_type=F32)
    s = s * jnp.exp(dlog - m_row)
    inter_w = jnp.exp(inter - m_row)
    num = _dot(s.astype(BF16), v.astype(BF16)) + inter_w * _dot(qb, ct_ref[h].astype(BF16))
    den = (jnp.sum(s, axis=-1, keepdims=True)
           + inter_w * jnp.sum(q * n_ref[h:h + 1, :], axis=-1, keepdims=True))
    hh = num / jnp.maximum(jnp.abs(den), jnp.exp(-m_row))

    g = b_i[t - 1:t, :]
    w = g - b_i + ig_i
    m_new = jnp.maximum(g + m_prev, jnp.max(w, axis=0, keepdims=True))
    decay = jnp.exp(g + m_prev - m_new)
    wk = jnp.exp(w - m_new)
    ct_ref[h] = decay * ct_ref[h] + lax.dot_general(
        kb, (v * wk).astype(BF16), (((0,), (0,)), ((), ())), preferred_element_type=F32)
    n_ref[h:h + 1, :] = decay * n_ref[h:h + 1, :] + jnp.sum(ks * wk, axis=0, keepdims=True)
    m_ref[h:h + 1, :] = jnp.broadcast_to(m_new, (1, m_ref.shape[1]))

    sl = slice(h * HEAD_DIM, (h + 1) * HEAD_DIM)
    mu = jnp.mean(hh, axis=-1, keepdims=True)
    var = jnp.mean(jnp.square(hh - mu), axis=-1, keepdims=True)
    hn = (hh - mu) * lax.rsqrt(var + EPS) * lnw_ref[:, sl]
    return (hn + skip_ref[:, sl] * xc_h) * jax.nn.silu(z_h)


def _mix_kernel(xm_ref, zm_ref, xr_ref, yr_ref, x1_ref, tri_ref,
                mcw_ref, mcb_ref, wqk_ref, wv_ref, wgc_ref, bgc_ref,
                lnw_ref, skip_ref,
                rcw_ref, rcb_ref, wax_ref, ba_ref, bx_ref, lam_ref,
                onm_ref, onr_ref, wout_ref,
                g1_ref, wg_ref, wu_ref, wd_ref, g2_ref,
                o_ref,
                xm_tail, xr_tail, ct_ref, n_ref, m_ref, hr_ref, x2_ref,
                *, tiles_per_seq):
    t = xm_ref.shape[0]
    step = pl.program_id(0)

    @pl.when(step == 0)
    def _init_handoff():
        x2_ref[...] = jnp.zeros_like(x2_ref)

    @pl.when(step % tiles_per_seq == 0)
    def _reset_state():
        xm_tail[...] = jnp.zeros_like(xm_tail)
        xr_tail[...] = jnp.zeros_like(xr_tail)
        ct_ref[...] = jnp.zeros_like(ct_ref)
        n_ref[...] = jnp.zeros_like(n_ref)
        m_ref[...] = jnp.full_like(m_ref, M_INIT)
        hr_ref[...] = jnp.zeros_like(hr_ref)

    ffn = _ChunkedFFN(x2_ref[(step + 1) % 2], g1_ref, wg_ref, wu_ref, wd_ref)

    xm = xm_ref[...]
    xc = jax.nn.silu(_causal_conv(xm_tail, xm, mcw_ref, mcb_ref))
    ffn.emit(1)
    xc_b = xc.astype(BF16)
    xm_b = xm.astype(BF16)
    gcol = jnp.broadcast_to(bgc_ref[...], (t, GATE_LANES))
    qs, ks, vs = [], [], []
    for h in range(HEADS):
        sl = slice(h * HEAD_DIM, (h + 1) * HEAD_DIM)
        qk = _dot(xc_b[:, sl], wqk_ref[h])
        q, k = qk[:, :HEAD_DIM], qk[:, HEAD_DIM:]
        v = _dot(xm_b[:, sl], wv_ref[h])
        for part, val in enumerate((q, k, v)):
            gcol = gcol + _dot(val.astype(BF16), wgc_ref[part, sl, :])
        qs.append(q)
        ks.append(k)
        vs.append(v)

    tri = tri_ref[...]
    causal = tri > 0.0
    bcol = jnp.dot(tri, jax.nn.log_sigmoid(gcol), precision=lax.Precision.HIGHEST,
                   preferred_element_type=F32)
    lane = lax.broadcasted_iota(jnp.int32, (1, GATE_LANES), 1)
    grow = jnp.where(lane < HEADS, gcol, bcol).T

    zm = zm_ref[...]
    outs = []
    for h in range(HEADS):
        sl = slice(h * HEAD_DIM, (h + 1) * HEAD_DIM)
        outs.append(_mlstm_head(h, qs[h], ks[h], vs[h], xc[:, sl], zm[:, sl],
                                gcol, bcol, grow, causal,
                                ct_ref, n_ref, m_ref, lnw_ref, skip_ref))
        ffn.emit(h % 2)
    out_m = jnp.concatenate(outs, axis=1)

    xrc = _causal_conv(xr_tail, xr_ref[...], rcw_ref, rcb_ref)
    xrc_b = xrc.astype(BF16)
    ffn.emit(1)
    pre = [_dot(xrc_b[:, g * HEAD_DIM:(g + 1) * HEAD_DIM], wax_ref[g]) for g in range(HEADS)]
    pre_a = jnp.concatenate([p[:, :HEAD_DIM] for p in pre], axis=1) + ba_ref[...]
    pre_x = jnp.concatenate([p[:, HEAD_DIM:] for p in pre], axis=1) + bx_ref[...]
    r = jax.nn.sigmoid(pre_a)
    i = jax.nn.sigmoid(pre_x)
    log_a = RGLRU_C * r * jax.nn.log_sigmoid(lam_ref[...])
    a = jnp.exp(log_a)
    u = jnp.sqrt(-jnp.tanh(log_a) * (1.0 + a * a)) * (i * xrc)
    ffn.emit(2)
    hseq = _linear_scan(a, u, hr_ref)
    ffn.emit(2)
    out_r = hseq * jax.nn.gelu(yr_ref[...])
    o_ref[...] = _time_unpermute(_rms(ffn.result(), g2_ref[...]))

    nm = _rms(out_m, onm_ref[...]).astype(BF16)
    nr = _rms(out_r, onr_ref[...]).astype(BF16)
    x2_ref[step % 2] = (x1_ref[...] + _dot(nm, wout_ref[0:D_MLSTM, :])
                        + _dot(nr, wout_ref[D_MLSTM:D_MLSTM + D_RGLRU, :]))


def _resident(shape):
    nd = len(shape)
    return pl.BlockSpec(shape, lambda *_: (0,) * nd, pipeline_mode=pl.Buffered(1))


def _blockdiag_dense(w):
    per = HEAD_DIM // QKV_BLOCK
    w = w.reshape(HEADS, per, QKV_BLOCK, QKV_BLOCK)
    eye = jnp.eye(per, dtype=w.dtype)
    return jnp.einsum('hnio,nm->hnimo', w, eye).reshape(HEADS, HEAD_DIM, HEAD_DIM)


def _row(v):
    return v.reshape(1, -1).astype(F32)


def _permuted_causal():
    rho = jnp.arange(TILE)
    time = (rho % SUBLANES) * SEG + rho // SUBLANES
    return (time[None, :] <= time[:, None]).astype(F32)


def kernel(x, norm_ffn1, ffn1_wg, ffn1_wu, ffn1_wd, norm_mix, w_in, m_conv_w, m_conv_b, m_wq, m_wk, m_wv, m_w_gates, m_b_gates, m_ln_w, m_skip, r_conv_w, r_conv_b, r_w_a, r_b_a, r_w_x, r_b_x, r_lam, out_norm_m, out_norm_r, w_out, norm_ffn2, ffn2_wg, ffn2_wu, ffn2_wd, norm_final):
    bsz, seq, _ = x.shape
    n_tok = bsz * seq
    assert norm_ffn1.shape[0] == 1, "single-layer block"
    assert seq % TILE == 0
    d_proj = w_in.shape[-1]
    n_tiles = n_tok // TILE
    xf = x.reshape(n_tok, D_MODEL)
    params = pltpu.CompilerParams(dimension_semantics=("arbitrary",), vmem_limit_bytes=VMEM_LIMIT)

    x1, proj = pl.pallas_call(
        _ffn_in_kernel,
        grid=(n_tiles,),
        in_specs=[
            pl.BlockSpec((TILE, D_MODEL), lambda i: (i, 0)),
            _resident((1, D_MODEL)),
            _resident((D_MODEL, D_FF)),
            _resident((D_MODEL, D_FF)),
            _resident((D_FF, D_MODEL)),
            _resident((1, D_MODEL)),
            _resident((D_MODEL, d_proj)),
        ],
        out_specs=[pl.BlockSpec((TILE, D_MODEL), lambda i: (i, 0)),
                   pl.BlockSpec((TILE, d_proj), lambda i: (i, 0))],
        out_shape=[jax.ShapeDtypeStruct((n_tok, D_MODEL), F32),
                   jax.ShapeDtypeStruct((n_tok, d_proj), F32)],
        compiler_params=params,
        name="ffn1_inproj",
    )(xf, _row(norm_ffn1[0]), ffn1_wg[0].astype(BF16), ffn1_wu[0].astype(BF16),
      ffn1_wd[0].astype(BF16), _row(norm_mix[0]), w_in[0].astype(BF16))

    wqk = jnp.concatenate([_blockdiag_dense(m_wq[0]), _blockdiag_dense(m_wk[0])], axis=-1).astype(BF16)
    wv = _blockdiag_dense(m_wv[0]).astype(BF16)
    wg3 = m_w_gates[0].reshape(3, D_MLSTM, N_GATES)
    wgc = jnp.pad(wg3, ((0, 0), (0, 0), (0, GATE_LANES - N_GATES))).astype(BF16)
    bgc = jnp.pad(m_b_gates[0].astype(F32), (0, GATE_LANES - N_GATES)).reshape(1, GATE_LANES)
    wax = jnp.concatenate([r_w_a[0], r_w_x[0]], axis=-1).astype(BF16)

    last = n_tiles - 1
    tile_spec = lambda j: pl.BlockSpec((TILE, D_MODEL), lambda i, j=j: (jnp.minimum(i, last), j))
    weights = [
        _permuted_causal(),
        m_conv_w[0].astype(F32), _row(m_conv_b[0]), wqk, wv, wgc, bgc,
        _row(m_ln_w[0]), _row(m_skip[0]),
        r_conv_w[0].astype(F32), _row(r_conv_b[0]), wax, _row(r_b_a[0]), _row(r_b_x[0]), _row(r_lam[0]),
        _row(out_norm_m[0]), _row(out_norm_r[0]), w_out[0].astype(BF16),
        _row(norm_ffn2[0]), ffn2_wg[0].astype(BF16), ffn2_wu[0].astype(BF16), ffn2_wd[0].astype(BF16),
        _row(norm_final),
    ]
    out = pl.pallas_call(
        functools.partial(_mix_kernel, tiles_per_seq=seq // TILE),
        grid=(n_tiles + 1,),
        in_specs=[tile_spec(0), tile_spec(1), tile_spec(2), tile_spec(3), tile_spec(0)]
                 + [_resident(w.shape) for w in weights],
        out_specs=pl.BlockSpec((TILE, D_MODEL), lambda i: (jnp.maximum(i - 1, 0), 0)),
        out_shape=jax.ShapeDtypeStruct((n_tok, D_MODEL), F32),
        scratch_shapes=[
            pltpu.VMEM((WRAP_ROWS, D_MLSTM), F32),
            pltpu.VMEM((WRAP_ROWS, D_RGLRU), F32),
            pltpu.VMEM((HEADS, HEAD_DIM, HEAD_DIM), F32),
            pltpu.VMEM((HEADS, HEAD_DIM), F32),
            pltpu.VMEM((HEADS, GATE_LANES), F32),
            pltpu.VMEM((SUBLANES, D_RGLRU), F32),
            pltpu.VMEM((2, TILE, D_MODEL), F32),
        ],
        compiler_params=params,
        name="mixers_ffn2",
    )(proj, proj, proj, proj, x1, *weights)
    return out.reshape(bsz, seq, D_MODEL)
```

```python
import functools

import jax
import jax.numpy as jnp
from jax import lax
from jax.experimental import pallas as pl
from jax.experimental.pallas import tpu as pltpu

D_MODEL = 1024
D_FF = 2816
D_MLSTM = 1024
D_RGLRU = 1024
HEADS = 4
HEAD_DIM = 256
QKV_BLOCK = 4
CONV_W = 4
RGLRU_C = 8.0
EPS = 1e-6
M_INIT = -1e30
N_GATES = 2 * HEADS

SUBLANES = 8
GATE_LANES = 128
TILE = 256
SEG = TILE // SUBLANES
WRAP_ROWS = (CONV_W - 1) * SUBLANES
FF_CHUNK = 256
N_FF_CHUNKS = D_FF // FF_CHUNK
VMEM_LIMIT = 56 * 1024 * 1024

F32 = jnp.float32
BF16 = jnp.bfloat16


def _time_permute(x):
    t, d = x.shape
    return x.reshape(SUBLANES, t // SUBLANES, d).swapaxes(0, 1).reshape(t, d)


def _time_unpermute(x):
    t, d = x.shape
    return x.reshape(t // SUBLANES, SUBLANES, d).swapaxes(0, 1).reshape(t, d)


def _rms(x, g):
    return x * lax.rsqrt(jnp.mean(x * x, axis=-1, keepdims=True) + EPS) * g


def _dot(a, b):
    return jnp.dot(a, b, preferred_element_type=F32)


def _col_block(i):
    return slice(i * HEAD_DIM, (i + 1) * HEAD_DIM)


def _ffn_in_kernel(x_ref, g1_ref, wg_ref, wu_ref, wd_ref, g2_ref, win_ref, x1_ref, proj_ref):
    x = _time_permute(x_ref[...])
    xn = _rms(x, g1_ref[...]).astype(BF16)
    h = (jax.nn.silu(_dot(xn, wg_ref[...])) * _dot(xn, wu_ref[...])).astype(BF16)
    y = x + 0.5 * _dot(h, wd_ref[...])
    x1_ref[...] = y
    proj_ref[...] = _dot(_rms(y, g2_ref[...]).astype(BF16), win_ref[...])


def _ffn_chunks(chunks, xn_ref, acc_ref, wg_ref, wu_ref, wd_ref):
    for c in chunks:
        sl = slice(c * FF_CHUNK, (c + 1) * FF_CHUNK)
        xn = xn_ref[...]
        h = (jax.nn.silu(_dot(xn, wg_ref[:, sl])) * _dot(xn, wu_ref[:, sl])).astype(BF16)
        part = _dot(h, wd_ref[sl, :])
        if c == 0:
            acc_ref[...] = part
        else:
            acc_ref[...] += part


def _causal_conv(tail_ref, x, w, bias, sl):
    t = x.shape[0]
    sub = lax.broadcasted_iota(jnp.int32, (SUBLANES, 1), 0)
    wrap = []
    for j in range(CONV_W - 1):
        cur = x[t - WRAP_ROWS + j * SUBLANES:t - WRAP_ROWS + (j + 1) * SUBLANES, :]
        prev = tail_ref[j * SUBLANES:(j + 1) * SUBLANES, sl]
        wrap.append(pltpu.roll(jnp.where(sub == SUBLANES - 1, prev, cur), 1, 0))
    tail_ref[:, sl] = x[t - WRAP_ROWS:t, :]
    ext = jnp.concatenate(wrap + [x], axis=0)
    y = bias
    for tap in range(CONV_W - 1):
        lo = WRAP_ROWS - (CONV_W - 1 - tap) * SUBLANES
        y = y + w[tap:tap + 1, :] * ext[lo:lo + t, :]
    return y + w[CONV_W - 1:CONV_W, :] * x


def _linear_scan(a, u, state_ref, sl):
    t = a.shape[0]
    hs, ps = [], []
    h = p = None
    for r in range(t // SUBLANES):
        a_r = a[r * SUBLANES:(r + 1) * SUBLANES, :]
        u_r = u[r * SUBLANES:(r + 1) * SUBLANES, :]
        h = u_r if h is None else a_r * h + u_r
        p = a_r if p is None else a_r * p
        hs.append(h)
        ps.append(p)
    sub = lax.broadcasted_iota(jnp.int32, (SUBLANES, 1), 0)
    a_seg, u_seg = p, h
    d = 1
    while d < SUBLANES:
        a_sh = jnp.where(sub < d, 1.0, pltpu.roll(a_seg, d, 0))
        u_sh = jnp.where(sub < d, 0.0, pltpu.roll(u_seg, d, 0))
        u_seg = a_seg * u_sh + u_seg
        a_seg = a_seg * a_sh
        d *= 2
    carry_in = state_ref[0:1, sl]
    seg_end = u_seg + a_seg * carry_in
    state_ref[0:1, sl] = seg_end[SUBLANES - 1:SUBLANES, :]
    seg_start = jnp.where(sub == 0, carry_in, pltpu.roll(seg_end, 1, 0))
    return jnp.concatenate([h_r + p_r * seg_start for h_r, p_r in zip(hs, ps)], axis=0)


def _cumulative_log_forget(tri, gcol):
    lf = jax.nn.log_sigmoid(gcol)
    hi = lf.astype(BF16)
    rest = lf - hi.astype(F32)
    mid = rest.astype(BF16)
    lo = (rest - mid.astype(F32)).astype(BF16)
    tri_b = tri.astype(BF16)
    return _dot(tri_b, hi) + _dot(tri_b, mid) + _dot(tri_b, lo)


def _mlstm_head(h, q, k, v, xc_h, z_h, gcol, bcol, grow, causal,
                ct_ref, n_ref, m_ref, lnw, skip):
    t = q.shape[0]
    qb = q.astype(BF16)
    ks = k * (HEAD_DIM ** -0.5)
    kb = ks.astype(BF16)
    b_i = bcol[:, HEADS + h:HEADS + h + 1]
    ig_i = gcol[:, h:h + 1]
    b_j = grow[HEADS + h:HEADS + h + 1, :]
    ig_j = grow[h:h + 1, :]
    m_prev = m_ref[h:h + 1, 0:1]

    dlog = jnp.where(causal, b_i - b_j + ig_j, -jnp.inf)
    inter = b_i + m_prev
    m_row = jnp.maximum(inter, jnp.max(dlog, axis=-1, keepdims=True))
    s = lax.dot_general(qb, kb, (((1,), (1,)), ((), ())), preferred_element_type=F32)
    s = s * jnp.exp(dlog - m_row)
    inter_w = jnp.exp(inter - m_row)
    num = _dot(s.astype(BF16), v.astype(BF16)) + inter_w * _dot(qb, ct_ref[h].astype(BF16))
    den = (jnp.sum(s, axis=-1, keepdims=True)
           + inter_w * jnp.sum(q * n_ref[h:h + 1, :], axis=-1, keepdims=True))
    hh = num / jnp.maximum(jnp.abs(den), jnp.exp(-m_row))

    g = b_i[t - 1:t, :]
    w = g - b_i + ig_i
    m_new = jnp.maximum(g + m_prev, jnp.max(w, axis=0, keepdims=True))
    decay = jnp.exp(g + m_prev - m_new)
    wk = jnp.exp(w - m_new)
    ct_ref[h] = decay * ct_ref[h] + lax.dot_general(
        kb, (v * wk).astype(BF16), (((0,), (0,)), ((), ())), preferred_element_type=F32)
    n_ref[h:h + 1, :] = decay * n_ref[h:h + 1, :] + jnp.sum(ks * wk, axis=0, keepdims=True)
    m_ref[h:h + 1, :] = jnp.broadcast_to(m_new, (1, m_ref.shape[1]))

    mu = jnp.mean(hh, axis=-1, keepdims=True)
    var = jnp.mean(jnp.square(hh - mu), axis=-1, keepdims=True)
    hn = (hh - mu) * lax.rsqrt(var + EPS) * lnw
    return (hn + skip * xc_h) * jax.nn.silu(z_h)


def _mix_kernel(xm_ref, zm_ref, xr_ref, yr_ref, x1_ref, tri_ref,
                mcw_ref, mcb_ref, wqk_ref, wv_ref, wgc_ref, bgc_ref,
                lnw_ref, skip_ref,
                rcw_ref, rcb_ref, wax_ref, ba_ref, bx_ref, lam_ref,
                onm_ref, onr_ref, wout_ref,
                g1_ref, wg_ref, wu_ref, wd_ref, g2_ref,
                o_ref,
                xm_tail, xr_tail, ct_ref, n_ref, m_ref, hr_ref, x2_ref,
                xn_ref, acc_ref, q_ref, k_ref, v_ref, xc_ref, xrc_ref, pre_ref, outm_ref, outr_ref,
                *, tiles_per_seq):
    t = xm_ref.shape[0]
    step = pl.program_id(0)
    prev = (step + 1) % 2
    ffn_refs = (xn_ref, acc_ref, wg_ref, wu_ref, wd_ref)

    @pl.when(step == 0)
    def _init_handoff():
        x2_ref[...] = jnp.zeros_like(x2_ref)

    @pl.when(step % tiles_per_seq == 0)
    def _reset_state():
        xm_tail[...] = jnp.zeros_like(xm_tail)
        xr_tail[...] = jnp.zeros_like(xr_tail)
        ct_ref[...] = jnp.zeros_like(ct_ref)
        n_ref[...] = jnp.zeros_like(n_ref)
        m_ref[...] = jnp.full_like(m_ref, M_INIT)
        hr_ref[...] = jnp.zeros_like(hr_ref)

    xn_ref[...] = _rms(x2_ref[prev], g1_ref[...]).astype(BF16)
    gcol = jnp.broadcast_to(bgc_ref[...], (t, GATE_LANES))
    for h in range(HEADS):
        sl = _col_block(h)
        xm = xm_ref[:, sl]
        xc = jax.nn.silu(_causal_conv(xm_tail, xm, mcw_ref[:, sl], mcb_ref[:, sl], sl))
        qk = _dot(xc.astype(BF16), wqk_ref[h])
        q, k = qk[:, :HEAD_DIM], qk[:, HEAD_DIM:]
        v = _dot(xm.astype(BF16), wv_ref[h])
        for part, val in enumerate((q, k, v)):
            gcol = gcol + _dot(val.astype(BF16), wgc_ref[part, sl, :])
        xc_ref[:, sl] = xc
        q_ref[:, sl] = q
        k_ref[:, sl] = k
        v_ref[:, sl] = v
    for g in range(HEADS):
        sl = _col_block(g)
        xrc = _causal_conv(xr_tail, xr_ref[:, sl], rcw_ref[:, sl], rcb_ref[:, sl], sl)
        xrc_ref[:, sl] = xrc
        pre_ref[:, 2 * g * HEAD_DIM:(2 * g + 2) * HEAD_DIM] = _dot(xrc.astype(BF16), wax_ref[g])
    _ffn_chunks((0, 1), *ffn_refs)

    tri = tri_ref[...]
    causal = tri > 0.0
    bcol = _cumulative_log_forget(tri, gcol)
    lane = lax.broadcasted_iota(jnp.int32, (1, GATE_LANES), 1)
    grow = jnp.where(lane < HEADS, gcol, bcol).T
    for h in range(HEADS):
        sl = _col_block(h)
        outm_ref[:, sl] = _mlstm_head(h, q_ref[:, sl], k_ref[:, sl], v_ref[:, sl], xc_ref[:, sl], zm_ref[:, sl],
                                      gcol, bcol, grow, causal, ct_ref, n_ref, m_ref,
                                      lnw_ref[:, sl], skip_ref[:, sl])
        if h % 2:
            _ffn_chunks((2, 3) if h == 1 else (4,), *ffn_refs)

    for g in range(HEADS):
        sl = _col_block(g)
        xrc = xrc_ref[:, sl]
        r = jax.nn.sigmoid(pre_ref[:, 2 * g * HEAD_DIM:(2 * g + 1) * HEAD_DIM] + ba_ref[:, sl])
        i = jax.nn.sigmoid(pre_ref[:, (2 * g + 1) * HEAD_DIM:(2 * g + 2) * HEAD_DIM] + bx_ref[:, sl])
        log_a = RGLRU_C * r * jax.nn.log_sigmoid(lam_ref[:, sl])
        a = jnp.exp(log_a)
        z = -jnp.tanh(log_a) * (1.0 + a * a)
        u = jnp.where(z > 0.0, z * lax.rsqrt(z), 0.0) * (i * xrc)
        hseq = _linear_scan(a, u, hr_ref, sl)
        outr_ref[:, sl] = hseq * jax.nn.gelu(yr_ref[:, sl])
        if g % 2:
            _ffn_chunks((5, 6, 7) if g == 1 else (8, 9, 10), *ffn_refs)

    o_ref[...] = _time_unpermute(_rms(x2_ref[prev] + 0.5 * acc_ref[...], g2_ref[...]))
    nm = _rms(outm_ref[...], onm_ref[...]).astype(BF16)
    nr = _rms(outr_ref[...], onr_ref[...]).astype(BF16)
    x2_ref[step % 2] = (x1_ref[...] + _dot(nm, wout_ref[0:D_MLSTM, :])
                        + _dot(nr, wout_ref[D_MLSTM:D_MLSTM + D_RGLRU, :]))


def _resident(shape):
    nd = len(shape)
    return pl.BlockSpec(shape, lambda *_: (0,) * nd, pipeline_mode=pl.Buffered(1))


def _blockdiag_dense(w):
    rows = w.reshape(HEADS, HEAD_DIM, QKV_BLOCK).astype(F32)
    col = jnp.arange(HEAD_DIM)
    spread = (col[None, :] % QKV_BLOCK == jnp.arange(QKV_BLOCK)[:, None]).astype(F32)
    dense = jnp.einsum('hro,oc->hrc', rows, spread, precision=lax.Precision.HIGHEST)
    same_block = col[:, None] // QKV_BLOCK == col[None, :] // QKV_BLOCK
    return jnp.where(same_block[None], dense, 0.0)


def _row(v):
    return v.reshape(1, -1).astype(F32)


def _permuted_causal():
    rho = jnp.arange(TILE)
    time = (rho % SUBLANES) * SEG + rho // SUBLANES
    return (time[None, :] <= time[:, None]).astype(F32)


def kernel(x, norm_ffn1, ffn1_wg, ffn1_wu, ffn1_wd, norm_mix, w_in, m_conv_w, m_conv_b, m_wq, m_wk, m_wv, m_w_gates, m_b_gates, m_ln_w, m_skip, r_conv_w, r_conv_b, r_w_a, r_b_a, r_w_x, r_b_x, r_lam, out_norm_m, out_norm_r, w_out, norm_ffn2, ffn2_wg, ffn2_wu, ffn2_wd, norm_final):
    bsz, seq, _ = x.shape
    n_tok = bsz * seq
    assert norm_ffn1.shape[0] == 1, "single-layer block"
    assert seq % TILE == 0
    d_proj = w_in.shape[-1]
    n_tiles = n_tok // TILE
    xf = x.reshape(n_tok, D_MODEL)
    params = pltpu.CompilerParams(dimension_semantics=("arbitrary",), vmem_limit_bytes=VMEM_LIMIT)

    x1, proj = pl.pallas_call(
        _ffn_in_kernel,
        grid=(n_tiles,),
        in_specs=[
            pl.BlockSpec((TILE, D_MODEL), lambda i: (i, 0)),
            _resident((1, D_MODEL)),
            _resident((D_MODEL, D_FF)),
            _resident((D_MODEL, D_FF)),
            _resident((D_FF, D_MODEL)),
            _resident((1, D_MODEL)),
            _resident((D_MODEL, d_proj)),
        ],
        out_specs=[pl.BlockSpec((TILE, D_MODEL), lambda i: (i, 0)),
                   pl.BlockSpec((TILE, d_proj), lambda i: (i, 0))],
        out_shape=[jax.ShapeDtypeStruct((n_tok, D_MODEL), F32),
                   jax.ShapeDtypeStruct((n_tok, d_proj), F32)],
        compiler_params=params,
        name="ffn1_inproj",
    )(xf, _row(norm_ffn1[0]), ffn1_wg[0].astype(BF16), ffn1_wu[0].astype(BF16),
      ffn1_wd[0].astype(BF16), _row(norm_mix[0]), w_in[0].astype(BF16))

    wqk = jnp.concatenate([_blockdiag_dense(m_wq[0]), _blockdiag_dense(m_wk[0])], axis=-1).astype(BF16)
    wv = _blockdiag_dense(m_wv[0]).astype(BF16)
    wg3 = m_w_gates[0].reshape(3, D_MLSTM, N_GATES)
    wgc = jnp.pad(wg3, ((0, 0), (0, 0), (0, GATE_LANES - N_GATES))).astype(BF16)
    bgc = jnp.pad(m_b_gates[0].astype(F32), (0, GATE_LANES - N_GATES)).reshape(1, GATE_LANES)
    wax = jnp.concatenate([r_w_a[0], r_w_x[0]], axis=-1).astype(BF16)

    last = n_tiles - 1
    tile_spec = lambda j: pl.BlockSpec((TILE, D_MODEL), lambda i, j=j: (jnp.minimum(i, last), j))
    weights = [
        _permuted_causal(),
        m_conv_w[0].astype(F32), _row(m_conv_b[0]), wqk, wv, wgc, bgc,
        _row(m_ln_w[0]), _row(m_skip[0]),
        r_conv_w[0].astype(F32), _row(r_conv_b[0]), wax, _row(r_b_a[0]), _row(r_b_x[0]), _row(r_lam[0]),
        _row(out_norm_m[0]), _row(out_norm_r[0]), w_out[0].astype(BF16),
        _row(norm_ffn2[0]), ffn2_wg[0].astype(BF16), ffn2_wu[0].astype(BF16), ffn2_wd[0].astype(BF16),
        _row(norm_final),
    ]
    tile_f32 = pltpu.VMEM((TILE, D_MODEL), F32)
    out = pl.pallas_call(
        functools.partial(_mix_kernel, tiles_per_seq=seq // TILE),
        grid=(n_tiles + 1,),
        in_specs=[tile_spec(0), tile_spec(1), tile_spec(2), tile_spec(3), tile_spec(0)]
                 + [_resident(w.shape) for w in weights],
        out_specs=pl.BlockSpec((TILE, D_MODEL), lambda i: (jnp.maximum(i - 1, 0), 0)),
        out_shape=jax.ShapeDtypeStruct((n_tok, D_MODEL), F32),
        scratch_shapes=[
            pltpu.VMEM((WRAP_ROWS, D_MLSTM), F32),
            pltpu.VMEM((WRAP_ROWS, D_RGLRU), F32),
            pltpu.VMEM((HEADS, HEAD_DIM, HEAD_DIM), F32),
            pltpu.VMEM((HEADS, HEAD_DIM), F32),
            pltpu.VMEM((HEADS, GATE_LANES), F32),
            pltpu.VMEM((SUBLANES, D_RGLRU), F32),
            pltpu.VMEM((2, TILE, D_MODEL), F32),
            pltpu.VMEM((TILE, D_MODEL), BF16),
            tile_f32,
            tile_f32, tile_f32, tile_f32,
            tile_f32,
            tile_f32,
            pltpu.VMEM((TILE, 2 * D_RGLRU), F32),
            tile_f32, tile_f32,
        ],
        compiler_params=params,
        name="mixers_ffn2",
    )(proj, proj, proj, proj, x1, *weights)
    return out.reshape(bsz, seq, D_MODEL)
```

```python
import functools

import jax
import jax.numpy as jnp
from jax import lax
from jax.experimental import pallas as pl
from jax.experimental.pallas import tpu as pltpu

D_MODEL = 1024
D_FF = 2816
D_MLSTM = 1024
D_RGLRU = 1024
HEADS = 4
HEAD_DIM = 256
QKV_BLOCK = 4
CONV_W = 4
RGLRU_C = 8.0
EPS = 1e-6
M_INIT = -1e30
N_GATES = 2 * HEADS

SUBLANES = 8
GATE_LANES = 128
TILE = 256
SEG = TILE // SUBLANES
WRAP_ROWS = (CONV_W - 1) * SUBLANES
FF_CHUNK = 256
N_FF_CHUNKS = D_FF // FF_CHUNK
VMEM_LIMIT = 60 * 1024 * 1024

F32 = jnp.float32
BF16 = jnp.bfloat16


def _time_permute(x):
    t, d = x.shape
    return x.reshape(SUBLANES, t // SUBLANES, d).swapaxes(0, 1).reshape(t, d)


def _time_unpermute(x):
    t, d = x.shape
    return x.reshape(t // SUBLANES, SUBLANES, d).swapaxes(0, 1).reshape(t, d)


def _rms(x, g):
    return x * lax.rsqrt(jnp.mean(x * x, axis=-1, keepdims=True) + EPS) * g


def _dot(a, b):
    return jnp.dot(a, b, preferred_element_type=F32)


def _col_block(i):
    return slice(i * HEAD_DIM, (i + 1) * HEAD_DIM)


def _ffn_in_kernel(x_ref, g1_ref, wg_ref, wu_ref, wd_ref, g2_ref, win_ref, x1_ref, proj_ref):
    x = _time_permute(x_ref[...])
    xn = _rms(x, g1_ref[...]).astype(BF16)
    h = (jax.nn.silu(_dot(xn, wg_ref[...])) * _dot(xn, wu_ref[...])).astype(BF16)
    y = x + 0.5 * _dot(h, wd_ref[...])
    x1_ref[...] = y
    proj_ref[...] = _dot(_rms(y, g2_ref[...]).astype(BF16), win_ref[...])


def _ffn_gate_up(chunks, xn_ref, gu_ref, wg_ref, wu_ref):
    for c in chunks:
        sl = slice(c * FF_CHUNK, (c + 1) * FF_CHUNK)
        xn = xn_ref[...]
        gu_ref[0, :, sl] = _dot(xn, wg_ref[:, sl])
        gu_ref[1, :, sl] = _dot(xn, wu_ref[:, sl])


def _ffn_act(chunks, gu_ref, h_ref):
    for c in chunks:
        sl = slice(c * FF_CHUNK, (c + 1) * FF_CHUNK)
        h_ref[:, sl] = (jax.nn.silu(gu_ref[0, :, sl]) * gu_ref[1, :, sl]).astype(BF16)


def _causal_conv(tail_ref, x, w, bias, sl):
    t = x.shape[0]
    sub = lax.broadcasted_iota(jnp.int32, (SUBLANES, 1), 0)
    wrap = []
    for j in range(CONV_W - 1):
        cur = x[t - WRAP_ROWS + j * SUBLANES:t - WRAP_ROWS + (j + 1) * SUBLANES, :]
        prev = tail_ref[j * SUBLANES:(j + 1) * SUBLANES, sl]
        wrap.append(pltpu.roll(jnp.where(sub == SUBLANES - 1, prev, cur), 1, 0))
    tail_ref[:, sl] = x[t - WRAP_ROWS:t, :]
    ext = jnp.concatenate(wrap + [x], axis=0)
    y = bias
    for tap in range(CONV_W - 1):
        lo = WRAP_ROWS - (CONV_W - 1 - tap) * SUBLANES
        y = y + w[tap:tap + 1, :] * ext[lo:lo + t, :]
    return y + w[CONV_W - 1:CONV_W, :] * x


def _linear_scan(a, u, state_ref, sl):
    t = a.shape[0]
    hs, ps = [], []
    h = p = None
    for r in range(t // SUBLANES):
        a_r = a[r * SUBLANES:(r + 1) * SUBLANES, :]
        u_r = u[r * SUBLANES:(r + 1) * SUBLANES, :]
        h = u_r if h is None else a_r * h + u_r
        p = a_r if p is None else a_r * p
        hs.append(h)
        ps.append(p)
    sub = lax.broadcasted_iota(jnp.int32, (SUBLANES, 1), 0)
    a_seg, u_seg = p, h
    d = 1
    while d < SUBLANES:
        a_sh = jnp.where(sub < d, 1.0, pltpu.roll(a_seg, d, 0))
        u_sh = jnp.where(sub < d, 0.0, pltpu.roll(u_seg, d, 0))
        u_seg = a_seg * u_sh + u_seg
        a_seg = a_seg * a_sh
        d *= 2
    carry_in = state_ref[0:1, sl]
    seg_end = u_seg + a_seg * carry_in
    state_ref[0:1, sl] = seg_end[SUBLANES - 1:SUBLANES, :]
    seg_start = jnp.where(sub == 0, carry_in, pltpu.roll(seg_end, 1, 0))
    return jnp.concatenate([h_r + p_r * seg_start for h_r, p_r in zip(hs, ps)], axis=0)


def _cumulative_log_forget(tri, gcol):
    lf = jax.nn.log_sigmoid(gcol)
    hi = lf.astype(BF16)
    rest = lf - hi.astype(F32)
    mid = rest.astype(BF16)
    lo = (rest - mid.astype(F32)).astype(BF16)
    tri_b = tri.astype(BF16)
    return _dot(tri_b, hi) + _dot(tri_b, mid) + _dot(tri_b, lo)


def _mlstm_head(h, q, k, v, xc_h, z_h, gcol, bcol, grow, causal,
                ct_ref, n_ref, m_ref, lnw, skip):
    t = q.shape[0]
    qb = q.astype(BF16)
    ks = k * (HEAD_DIM ** -0.5)
    kb = ks.astype(BF16)
    b_i = bcol[:, HEADS + h:HEADS + h + 1]
    ig_i = gcol[:, h:h + 1]
    b_j = grow[HEADS + h:HEADS + h + 1, :]
    ig_j = grow[h:h + 1, :]
    m_prev = m_ref[h:h + 1, 0:1]

    dlog = jnp.where(causal, b_i - b_j + ig_j, -jnp.inf)
    inter = b_i + m_prev
    m_row = jnp.maximum(inter, jnp.max(dlog, axis=-1, keepdims=True))
    s = lax.dot_general(qb, kb, (((1,), (1,)), ((), ())), preferred_element_type=F32)
    s = s * jnp.exp(dlog - m_row)
    inter_w = jnp.exp(inter - m_row)
    num = _dot(s.astype(BF16), v.astype(BF16)) + inter_w * _dot(qb, ct_ref[h].astype(BF16))
    den = (jnp.sum(s, axis=-1, keepdims=True)
           + inter_w * jnp.sum(q * n_ref[h:h + 1, :], axis=-1, keepdims=True))
    hh = num / jnp.maximum(jnp.abs(den), jnp.exp(-m_row))

    g = b_i[t - 1:t, :]
    w = g - b_i + ig_i
    m_new = jnp.maximum(g + m_prev, jnp.max(w, axis=0, keepdims=True))
    decay = jnp.exp(g + m_prev - m_new)
    wk = jnp.exp(w - m_new)
    ct_ref[h] = decay * ct_ref[h] + lax.dot_general(
        kb, (v * wk).astype(BF16), (((0,), (0,)), ((), ())), preferred_element_type=F32)
    n_ref[h:h + 1, :] = decay * n_ref[h:h + 1, :] + jnp.sum(ks * wk, axis=0, keepdims=True)
    m_ref[h:h + 1, :] = jnp.broadcast_to(m_new, (1, m_ref.shape[1]))

    mu = jnp.mean(hh, axis=-1, keepdims=True)
    var = jnp.mean(jnp.square(hh - mu), axis=-1, keepdims=True)
    hn = (hh - mu) * lax.rsqrt(var + EPS) * lnw
    return (hn + skip * xc_h) * jax.nn.silu(z_h)


def _mix_kernel(xm_ref, zm_ref, xr_ref, yr_ref, x1_ref, tri_ref,
                mcw_ref, mcb_ref, wqk_ref, wv_ref, wgc_ref, bgc_ref,
                lnw_ref, skip_ref,
                rcw_ref, rcb_ref, wax_ref, ba_ref, bx_ref, lam_ref,
                onm_ref, onr_ref, wout_ref,
                g1_ref, wg_ref, wu_ref, wd_ref, g2_ref,
                o_ref,
                xm_tail, xr_tail, ct_ref, n_ref, m_ref, hr_ref, x2_ref,
                xn_ref, gu_ref, h_ref, q_ref, k_ref, v_ref, xc_ref, xrc_ref, pre_ref, outm_ref, outr_ref,
                *, tiles_per_seq):
    t = xm_ref.shape[0]
    step = pl.program_id(0)
    prev = (step + 1) % 2

    @pl.when(step == 0)
    def _init_handoff():
        x2_ref[...] = jnp.zeros_like(x2_ref)

    @pl.when(step % tiles_per_seq == 0)
    def _reset_state():
        xm_tail[...] = jnp.zeros_like(xm_tail)
        xr_tail[...] = jnp.zeros_like(xr_tail)
        ct_ref[...] = jnp.zeros_like(ct_ref)
        n_ref[...] = jnp.zeros_like(n_ref)
        m_ref[...] = jnp.full_like(m_ref, M_INIT)
        hr_ref[...] = jnp.zeros_like(hr_ref)

    xn_ref[...] = _rms(x2_ref[prev], g1_ref[...]).astype(BF16)
    gcol = jnp.broadcast_to(bgc_ref[...], (t, GATE_LANES))
    for h in range(HEADS):
        _ffn_gate_up((h,), xn_ref, gu_ref, wg_ref, wu_ref)
        sl = _col_block(h)
        xm = xm_ref[:, sl]
        xc = jax.nn.silu(_causal_conv(xm_tail, xm, mcw_ref[:, sl], mcb_ref[:, sl], sl))
        qk = _dot(xc.astype(BF16), wqk_ref[h])
        q, k = qk[:, :HEAD_DIM], qk[:, HEAD_DIM:]
        v = _dot(xm.astype(BF16), wv_ref[h])
        for part, val in enumerate((q, k, v)):
            gcol = gcol + _dot(val.astype(BF16), wgc_ref[part, sl, :])
        xc_ref[:, sl] = xc
        q_ref[:, sl] = q
        k_ref[:, sl] = k
        v_ref[:, sl] = v
    for g in range(HEADS):
        _ffn_gate_up((HEADS + g,), xn_ref, gu_ref, wg_ref, wu_ref)
        sl = _col_block(g)
        xrc = _causal_conv(xr_tail, xr_ref[:, sl], rcw_ref[:, sl], rcb_ref[:, sl], sl)
        xrc_ref[:, sl] = xrc
        pre_ref[:, 2 * g * HEAD_DIM:(2 * g + 2) * HEAD_DIM] = _dot(xrc.astype(BF16), wax_ref[g])

    tri = tri_ref[...]
    causal = tri > 0.0
    bcol = _cumulative_log_forget(tri, gcol)
    lane = lax.broadcasted_iota(jnp.int32, (1, GATE_LANES), 1)
    grow = jnp.where(lane < HEADS, gcol, bcol).T
    for h in range(HEADS):
        if h < N_FF_CHUNKS - 2 * HEADS:
            _ffn_gate_up((2 * HEADS + h,), xn_ref, gu_ref, wg_ref, wu_ref)
        sl = _col_block(h)
        outm_ref[:, sl] = _mlstm_head(h, q_ref[:, sl], k_ref[:, sl], v_ref[:, sl], xc_ref[:, sl], zm_ref[:, sl],
                                      gcol, bcol, grow, causal, ct_ref, n_ref, m_ref,
                                      lnw_ref[:, sl], skip_ref[:, sl])
        _ffn_act(((0, 1, 2), (3, 4, 5), (6, 7), (8, 9, 10))[h], gu_ref, h_ref)

    downs = []
    for g in range(HEADS):
        sl = _col_block(g)
        downs.append(_dot(h_ref[...], wd_ref[:, sl]))
        xrc = xrc_ref[:, sl]
        r = jax.nn.sigmoid(pre_ref[:, 2 * g * HEAD_DIM:(2 * g + 1) * HEAD_DIM] + ba_ref[:, sl])
        i = jax.nn.sigmoid(pre_ref[:, (2 * g + 1) * HEAD_DIM:(2 * g + 2) * HEAD_DIM] + bx_ref[:, sl])
        log_a = RGLRU_C * r * jax.nn.log_sigmoid(lam_ref[:, sl])
        a = jnp.exp(log_a)
        z = -jnp.tanh(log_a) * (1.0 + a * a)
        u = jnp.where(z > 0.0, z * lax.rsqrt(z), 0.0) * (i * xrc)
        hseq = _linear_scan(a, u, hr_ref, sl)
        outr_ref[:, sl] = hseq * jax.nn.gelu(yr_ref[:, sl])

    o_ref[...] = _time_unpermute(_rms(x2_ref[prev] + 0.5 * jnp.concatenate(downs, axis=1), g2_ref[...]))
    nm = _rms(outm_ref[...], onm_ref[...]).astype(BF16)
    nr = _rms(outr_ref[...], onr_ref[...]).astype(BF16)
    x2_ref[step % 2] = (x1_ref[...] + _dot(nm, wout_ref[0:D_MLSTM, :])
                        + _dot(nr, wout_ref[D_MLSTM:D_MLSTM + D_RGLRU, :]))


def _resident(shape):
    nd = len(shape)
    return pl.BlockSpec(shape, lambda *_: (0,) * nd, pipeline_mode=pl.Buffered(1))


def _blockdiag_dense(w):
    rows = w.reshape(HEADS, HEAD_DIM, QKV_BLOCK).astype(F32)
    col = jnp.arange(HEAD_DIM)
    spread = (col[None, :] % QKV_BLOCK == jnp.arange(QKV_BLOCK)[:, None]).astype(F32)
    dense = jnp.einsum('hro,oc->hrc', rows, spread, precision=lax.Precision.HIGHEST)
    same_block = col[:, None] // QKV_BLOCK == col[None, :] // QKV_BLOCK
    return jnp.where(same_block[None], dense, 0.0)


def _row(v):
    return v.reshape(1, -1).astype(F32)


def _permuted_causal():
    rho = jnp.arange(TILE)
    time = (rho % SUBLANES) * SEG + rho // SUBLANES
    return (time[None, :] <= time[:, None]).astype(F32)


def kernel(x, norm_ffn1, ffn1_wg, ffn1_wu, ffn1_wd, norm_mix, w_in, m_conv_w, m_conv_b, m_wq, m_wk, m_wv, m_w_gates, m_b_gates, m_ln_w, m_skip, r_conv_w, r_conv_b, r_w_a, r_b_a, r_w_x, r_b_x, r_lam, out_norm_m, out_norm_r, w_out, norm_ffn2, ffn2_wg, ffn2_wu, ffn2_wd, norm_final):
    bsz, seq, _ = x.shape
    n_tok = bsz * seq
    assert norm_ffn1.shape[0] == 1, "single-layer block"
    assert seq % TILE == 0
    d_proj = w_in.shape[-1]
    n_tiles = n_tok // TILE
    xf = x.reshape(n_tok, D_MODEL)
    params = pltpu.CompilerParams(dimension_semantics=("arbitrary",), vmem_limit_bytes=VMEM_LIMIT)

    x1, proj = pl.pallas_call(
        _ffn_in_kernel,
        grid=(n_tiles,),
        in_specs=[
            pl.BlockSpec((TILE, D_MODEL), lambda i: (i, 0)),
            _resident((1, D_MODEL)),
            _resident((D_MODEL, D_FF)),
            _resident((D_MODEL, D_FF)),
            _resident((D_FF, D_MODEL)),
            _resident((1, D_MODEL)),
            _resident((D_MODEL, d_proj)),
        ],
        out_specs=[pl.BlockSpec((TILE, D_MODEL), lambda i: (i, 0)),
                   pl.BlockSpec((TILE, d_proj), lambda i: (i, 0))],
        out_shape=[jax.ShapeDtypeStruct((n_tok, D_MODEL), F32),
                   jax.ShapeDtypeStruct((n_tok, d_proj), F32)],
        compiler_params=params,
        name="ffn1_inproj",
    )(xf, _row(norm_ffn1[0]), ffn1_wg[0].astype(BF16), ffn1_wu[0].astype(BF16),
      ffn1_wd[0].astype(BF16), _row(norm_mix[0]), w_in[0].astype(BF16))

    wqk = jnp.concatenate([_blockdiag_dense(m_wq[0]), _blockdiag_dense(m_wk[0])], axis=-1).astype(BF16)
    wv = _blockdiag_dense(m_wv[0]).astype(BF16)
    wg3 = m_w_gates[0].reshape(3, D_MLSTM, N_GATES)
    wgc = jnp.pad(wg3, ((0, 0), (0, 0), (0, GATE_LANES - N_GATES))).astype(BF16)
    bgc = jnp.pad(m_b_gates[0].astype(F32), (0, GATE_LANES - N_GATES)).reshape(1, GATE_LANES)
    wax = jnp.concatenate([r_w_a[0], r_w_x[0]], axis=-1).astype(BF16)

    last = n_tiles - 1
    tile_spec = lambda j: pl.BlockSpec((TILE, D_MODEL), lambda i, j=j: (jnp.minimum(i, last), j))
    weights = [
        _permuted_causal(),
        m_conv_w[0].astype(F32), _row(m_conv_b[0]), wqk, wv, wgc, bgc,
        _row(m_ln_w[0]), _row(m_skip[0]),
        r_conv_w[0].astype(F32), _row(r_conv_b[0]), wax, _row(r_b_a[0]), _row(r_b_x[0]), _row(r_lam[0]),
        _row(out_norm_m[0]), _row(out_norm_r[0]), w_out[0].astype(BF16),
        _row(norm_ffn2[0]), ffn2_wg[0].astype(BF16), ffn2_wu[0].astype(BF16), ffn2_wd[0].astype(BF16),
        _row(norm_final),
    ]
    tile_f32 = pltpu.VMEM((TILE, D_MODEL), F32)
    out = pl.pallas_call(
        functools.partial(_mix_kernel, tiles_per_seq=seq // TILE),
        grid=(n_tiles + 1,),
        in_specs=[tile_spec(0), tile_spec(1), tile_spec(2), tile_spec(3), tile_spec(0)]
                 + [_resident(w.shape) for w in weights],
        out_specs=pl.BlockSpec((TILE, D_MODEL), lambda i: (jnp.maximum(i - 1, 0), 0)),
        out_shape=jax.ShapeDtypeStruct((n_tok, D_MODEL), F32),
        scratch_shapes=[
            pltpu.VMEM((WRAP_ROWS, D_MLSTM), F32),
            pltpu.VMEM((WRAP_ROWS, D_RGLRU), F32),
            pltpu.VMEM((HEADS, HEAD_DIM, HEAD_DIM), F32),
            pltpu.VMEM((HEADS, HEAD_DIM), F32),
            pltpu.VMEM((HEADS, GATE_LANES), F32),
            pltpu.VMEM((SUBLANES, D_RGLRU), F32),
            pltpu.VMEM((2, TILE, D_MODEL), F32),
            pltpu.VMEM((TILE, D_MODEL), BF16),
            pltpu.VMEM((2, TILE, D_FF), F32),
            pltpu.VMEM((TILE, D_FF), BF16),
            tile_f32, tile_f32, tile_f32,
            tile_f32,
            tile_f32,
            pltpu.VMEM((TILE, 2 * D_RGLRU), F32),
            tile_f32, tile_f32,
        ],
        compiler_params=params,
        name="mixers_ffn2",
    )(proj, proj, proj, proj, x1, *weights)
    return out.reshape(bsz, seq, D_MODEL)
```

```python
import functools

import jax
import jax.numpy as jnp
from jax import lax
from jax.experimental import pallas as pl
from jax.experimental.pallas import tpu as pltpu

D_MODEL = 1024
D_FF = 2816
D_MLSTM = 1024
D_RGLRU = 1024
HEADS = 4
HEAD_DIM = 256
QKV_BLOCK = 4
CONV_W = 4
RGLRU_C = 8.0
EPS = 1e-6
M_INIT = -1e30
N_GATES = 2 * HEADS

SUBLANES = 8
GATE_LANES = 128
TILE = 256
SEG = TILE // SUBLANES
WRAP_ROWS = (CONV_W - 1) * SUBLANES
FF_CHUNK = 256
N_FF_CHUNKS = D_FF // FF_CHUNK
VMEM_LIMIT = 60 * 1024 * 1024

F32 = jnp.float32
BF16 = jnp.bfloat16


def _time_permute(x):
    t, d = x.shape
    return x.reshape(SUBLANES, t // SUBLANES, d).swapaxes(0, 1).reshape(t, d)


def _time_unpermute(x):
    t, d = x.shape
    return x.reshape(t // SUBLANES, SUBLANES, d).swapaxes(0, 1).reshape(t, d)


def _rms(x, g):
    return x * lax.rsqrt(jnp.mean(x * x, axis=-1, keepdims=True) + EPS) * g


def _dot(a, b):
    return jnp.dot(a, b, preferred_element_type=F32)


def _col_block(i):
    return slice(i * HEAD_DIM, (i + 1) * HEAD_DIM)


def _ffn_in_kernel(x_ref, g1_ref, wg_ref, wu_ref, wd_ref, g2_ref, win_ref, x1_ref, proj_ref):
    x = _time_permute(x_ref[...])
    xn = _rms(x, g1_ref[...]).astype(BF16)
    h = (jax.nn.silu(_dot(xn, wg_ref[...])) * _dot(xn, wu_ref[...])).astype(BF16)
    y = x + 0.5 * _dot(h, wd_ref[...])
    x1_ref[...] = y
    proj_ref[...] = _dot(_rms(y, g2_ref[...]).astype(BF16), win_ref[...])


def _ffn_gate_up(chunks, xn_ref, gu_ref, wg_ref, wu_ref):
    for c in chunks:
        sl = slice(c * FF_CHUNK, (c + 1) * FF_CHUNK)
        xn = xn_ref[...]
        gu_ref[0, :, sl] = _dot(xn, wg_ref[:, sl])
        gu_ref[1, :, sl] = _dot(xn, wu_ref[:, sl])


def _ffn_act(chunks, gu_ref, h_ref):
    for c in chunks:
        sl = slice(c * FF_CHUNK, (c + 1) * FF_CHUNK)
        h_ref[:, sl] = (jax.nn.silu(gu_ref[0, :, sl]) * gu_ref[1, :, sl]).astype(BF16)


def _causal_conv(tail_ref, x, w, bias, sl):
    t = x.shape[0]
    sub = lax.broadcasted_iota(jnp.int32, (SUBLANES, 1), 0)
    wrap = []
    for j in range(CONV_W - 1):
        cur = x[t - WRAP_ROWS + j * SUBLANES:t - WRAP_ROWS + (j + 1) * SUBLANES, :]
        prev = tail_ref[j * SUBLANES:(j + 1) * SUBLANES, sl]
        wrap.append(pltpu.roll(jnp.where(sub == SUBLANES - 1, prev, cur), 1, 0))
    tail_ref[:, sl] = x[t - WRAP_ROWS:t, :]
    ext = jnp.concatenate(wrap + [x], axis=0)
    y = bias
    for tap in range(CONV_W - 1):
        lo = WRAP_ROWS - (CONV_W - 1 - tap) * SUBLANES
        y = y + w[tap:tap + 1, :] * ext[lo:lo + t, :]
    return y + w[CONV_W - 1:CONV_W, :] * x


def _linear_scan(a, u, state_ref, sl):
    t = a.shape[0]
    hs, ps = [], []
    h = p = None
    for r in range(t // SUBLANES):
        a_r = a[r * SUBLANES:(r + 1) * SUBLANES, :]
        u_r = u[r * SUBLANES:(r + 1) * SUBLANES, :]
        h = u_r if h is None else a_r * h + u_r
        p = a_r if p is None else a_r * p
        hs.append(h)
        ps.append(p)
    sub = lax.broadcasted_iota(jnp.int32, (SUBLANES, 1), 0)
    a_seg, u_seg = p, h
    d = 1
    while d < SUBLANES:
        a_sh = jnp.where(sub < d, 1.0, pltpu.roll(a_seg, d, 0))
        u_sh = jnp.where(sub < d, 0.0, pltpu.roll(u_seg, d, 0))
        u_seg = a_seg * u_sh + u_seg
        a_seg = a_seg * a_sh
        d *= 2
    carry_in = state_ref[0:1, sl]
    seg_end = u_seg + a_seg * carry_in
    state_ref[0:1, sl] = seg_end[SUBLANES - 1:SUBLANES, :]
    seg_start = jnp.where(sub == 0, carry_in, pltpu.roll(seg_end, 1, 0))
    return jnp.concatenate([h_r + p_r * seg_start for h_r, p_r in zip(hs, ps)], axis=0)


def _cumulative_log_forget(tri, gcol):
    lf = jax.nn.log_sigmoid(gcol)
    hi = lf.astype(BF16)
    rest = lf - hi.astype(F32)
    mid = rest.astype(BF16)
    lo = (rest - mid.astype(F32)).astype(BF16)
    tri_b = tri.astype(BF16)
    return _dot(tri_b, hi) + _dot(tri_b, mid) + _dot(tri_b, lo)


def _mlstm_head(h, q, k, v, xc_h, z_h, gcol, bcol, grow, causal,
                ct_ref, n_ref, m_ref, lnw, skip):
    t = q.shape[0]
    qb = q.astype(BF16)
    ks = k * (HEAD_DIM ** -0.5)
    kb = ks.astype(BF16)
    b_i = bcol[:, HEADS + h:HEADS + h + 1]
    ig_i = gcol[:, h:h + 1]
    b_j = grow[HEADS + h:HEADS + h + 1, :]
    ig_j = grow[h:h + 1, :]
    m_prev = m_ref[h:h + 1, 0:1]

    dlog = jnp.where(causal, b_i - b_j + ig_j, -jnp.inf)
    inter = b_i + m_prev
    m_row = jnp.maximum(inter, jnp.max(dlog, axis=-1, keepdims=True))
    s = lax.dot_general(qb, kb, (((1,), (1,)), ((), ())), preferred_element_type=F32)
    s = s * jnp.exp(dlog - m_row)
    inter_w = jnp.exp(inter - m_row)
    num = _dot(s.astype(BF16), v.astype(BF16)) + inter_w * _dot(qb, ct_ref[h].astype(BF16))
    den = (jnp.sum(s, axis=-1, keepdims=True)
           + inter_w * jnp.sum(q * n_ref[h:h + 1, :], axis=-1, keepdims=True))
    hh = num / jnp.maximum(jnp.abs(den), jnp.exp(-m_row))

    g = b_i[t - 1:t, :]
    w = g - b_i + ig_i
    m_new = jnp.maximum(g + m_prev, jnp.max(w, axis=0, keepdims=True))
    decay = jnp.exp(g + m_prev - m_new)
    wk = jnp.exp(w - m_new)
    ct_ref[h] = decay * ct_ref[h] + lax.dot_general(
        kb, (v * wk).astype(BF16), (((0,), (0,)), ((), ())), preferred_element_type=F32)
    n_ref[h:h + 1, :] = decay * n_ref[h:h + 1, :] + jnp.sum(ks * wk, axis=0, keepdims=True)
    m_ref[h:h + 1, :] = jnp.broadcast_to(m_new, (1, m_ref.shape[1]))

    mu = jnp.mean(hh, axis=-1, keepdims=True)
    var = jnp.mean(jnp.square(hh - mu), axis=-1, keepdims=True)
    hn = (hh - mu) * lax.rsqrt(var + EPS) * lnw
    return (hn + skip * xc_h) * jax.nn.silu(z_h)


def _mix_kernel(xm_ref, zm_ref, xr_ref, yr_ref, x1_ref, tri_ref,
                mcw_ref, mcb_ref, wqk_ref, wv_ref, wgc_ref, bgc_ref,
                lnw_ref, skip_ref,
                rcw_ref, rcb_ref, wax_ref, ba_ref, bx_ref, lam_ref,
                onm_ref, onr_ref, wout_ref,
                g1_ref, wg_ref, wu_ref, wd_ref, g2_ref,
                o_ref,
                xm_tail, xr_tail, ct_ref, n_ref, m_ref, hr_ref, x2_ref,
                xn_ref, gu_ref, h_ref, q_ref, k_ref, v_ref, xc_ref, xrc_ref, pre_ref, outm_ref, outr_ref,
                *, tiles_per_seq):
    t = xm_ref.shape[0]
    step = pl.program_id(0)
    prev = (step + 1) % 2

    @pl.when(step == 0)
    def _init_handoff():
        x2_ref[...] = jnp.zeros_like(x2_ref)

    @pl.when(step % tiles_per_seq == 0)
    def _reset_state():
        xm_tail[...] = jnp.zeros_like(xm_tail)
        xr_tail[...] = jnp.zeros_like(xr_tail)
        ct_ref[...] = jnp.zeros_like(ct_ref)
        n_ref[...] = jnp.zeros_like(n_ref)
        m_ref[...] = jnp.full_like(m_ref, M_INIT)
        hr_ref[...] = jnp.zeros_like(hr_ref)

    xn_ref[...] = _rms(x2_ref[prev], g1_ref[...]).astype(BF16)
    gcol = jnp.broadcast_to(bgc_ref[...], (t, GATE_LANES))
    for h in range(HEADS):
        if h:
            _ffn_gate_up((h - 1,), xn_ref, gu_ref, wg_ref, wu_ref)
        sl = _col_block(h)
        xm = xm_ref[:, sl]
        xc = jax.nn.silu(_causal_conv(xm_tail, xm, mcw_ref[:, sl], mcb_ref[:, sl], sl))
        qk = _dot(xc.astype(BF16), wqk_ref[h])
        q, k = qk[:, :HEAD_DIM], qk[:, HEAD_DIM:]
        v = _dot(xm.astype(BF16), wv_ref[h])
        for part, val in enumerate((q, k, v)):
            gcol = gcol + _dot(val.astype(BF16), wgc_ref[part, sl, :])
        xc_ref[:, sl] = xc
        q_ref[:, sl] = q
        k_ref[:, sl] = k
        v_ref[:, sl] = v
    for g in range(HEADS):
        _ffn_gate_up((HEADS - 1 + g,), xn_ref, gu_ref, wg_ref, wu_ref)
        sl = _col_block(g)
        xrc = _causal_conv(xr_tail, xr_ref[:, sl], rcw_ref[:, sl], rcb_ref[:, sl], sl)
        xrc_ref[:, sl] = xrc
        pre_ref[:, 2 * g * HEAD_DIM:(2 * g + 2) * HEAD_DIM] = _dot(xrc.astype(BF16), wax_ref[g])

    tri = tri_ref[...]
    causal = tri > 0.0
    bcol = _cumulative_log_forget(tri, gcol)
    lane = lax.broadcasted_iota(jnp.int32, (1, GATE_LANES), 1)
    grow = jnp.where(lane < HEADS, gcol, bcol).T
    _ffn_gate_up(range(2 * HEADS - 1, N_FF_CHUNKS), xn_ref, gu_ref, wg_ref, wu_ref)
    for h in range(HEADS):
        sl = _col_block(h)
        outm_ref[:, sl] = _mlstm_head(h, q_ref[:, sl], k_ref[:, sl], v_ref[:, sl], xc_ref[:, sl], zm_ref[:, sl],
                                      gcol, bcol, grow, causal, ct_ref, n_ref, m_ref,
                                      lnw_ref[:, sl], skip_ref[:, sl])
        _ffn_act(((0, 1, 2, 3), (4, 5, 6), (7, 8), (9, 10))[h], gu_ref, h_ref)

    downs = []
    for g in range(HEADS):
        sl = _col_block(g)
        downs.append(_dot(h_ref[...], wd_ref[:, sl]))
        xrc = xrc_ref[:, sl]
        r = jax.nn.sigmoid(pre_ref[:, 2 * g * HEAD_DIM:(2 * g + 1) * HEAD_DIM] + ba_ref[:, sl])
        i = jax.nn.sigmoid(pre_ref[:, (2 * g + 1) * HEAD_DIM:(2 * g + 2) * HEAD_DIM] + bx_ref[:, sl])
        log_a = RGLRU_C * r * jax.nn.log_sigmoid(lam_ref[:, sl])
        a = jnp.exp(log_a)
        z = -jnp.tanh(log_a) * (1.0 + a * a)
        u = jnp.where(z > 0.0, z * lax.rsqrt(z), 0.0) * (i * xrc)
        hseq = _linear_scan(a, u, hr_ref, sl)
        outr_ref[:, sl] = hseq * jax.nn.gelu(yr_ref[:, sl])

    o_ref[...] = _time_unpermute(_rms(x2_ref[prev] + 0.5 * jnp.concatenate(downs, axis=1), g2_ref[...]))
    nm = _rms(outm_ref[...], onm_ref[...]).astype(BF16)
    nr = _rms(outr_ref[...], onr_ref[...]).astype(BF16)
    x2_ref[step % 2] = (x1_ref[...] + _dot(nm, wout_ref[0:D_MLSTM, :])
                        + _dot(nr, wout_ref[D_MLSTM:D_MLSTM + D_RGLRU, :]))


def _resident(shape):
    nd = len(shape)
    return pl.BlockSpec(shape, lambda *_: (0,) * nd, pipeline_mode=pl.Buffered(1))


def _blockdiag_dense(w):
    rows = w.reshape(HEADS, HEAD_DIM, QKV_BLOCK).astype(F32)
    col = jnp.arange(HEAD_DIM)
    spread = (col[None, :] % QKV_BLOCK == jnp.arange(QKV_BLOCK)[:, None]).astype(F32)
    dense = jnp.einsum('hro,oc->hrc', rows, spread, precision=lax.Precision.HIGHEST)
    same_block = col[:, None] // QKV_BLOCK == col[None, :] // QKV_BLOCK
    return jnp.where(same_block[None], dense, 0.0)


def _row(v):
    return v.reshape(1, -1).astype(F32)


def _permuted_causal():
    rho = jnp.arange(TILE)
    time = (rho % SUBLANES) * SEG + rho // SUBLANES
    return (time[None, :] <= time[:, None]).astype(F32)


def kernel(x, norm_ffn1, ffn1_wg, ffn1_wu, ffn1_wd, norm_mix, w_in, m_conv_w, m_conv_b, m_wq, m_wk, m_wv, m_w_gates, m_b_gates, m_ln_w, m_skip, r_conv_w, r_conv_b, r_w_a, r_b_a, r_w_x, r_b_x, r_lam, out_norm_m, out_norm_r, w_out, norm_ffn2, ffn2_wg, ffn2_wu, ffn2_wd, norm_final):
    bsz, seq, _ = x.shape
    n_tok = bsz * seq
    assert norm_ffn1.shape[0] == 1, "single-layer block"
    assert seq % TILE == 0
    d_proj = w_in.shape[-1]
    n_tiles = n_tok // TILE
    xf = x.reshape(n_tok, D_MODEL)
    params = pltpu.CompilerParams(dimension_semantics=("arbitrary",), vmem_limit_bytes=VMEM_LIMIT)

    x1, proj = pl.pallas_call(
        _ffn_in_kernel,
        grid=(n_tiles,),
        in_specs=[
            pl.BlockSpec((TILE, D_MODEL), lambda i: (i, 0)),
            _resident((1, D_MODEL)),
            _resident((D_MODEL, D_FF)),
            _resident((D_MODEL, D_FF)),
            _resident((D_FF, D_MODEL)),
            _resident((1, D_MODEL)),
            _resident((D_MODEL, d_proj)),
        ],
        out_specs=[pl.BlockSpec((TILE, D_MODEL), lambda i: (i, 0)),
                   pl.BlockSpec((TILE, d_proj), lambda i: (i, 0))],
        out_shape=[jax.ShapeDtypeStruct((n_tok, D_MODEL), F32),
                   jax.ShapeDtypeStruct((n_tok, d_proj), F32)],
        compiler_params=params,
        name="ffn1_inproj",
    )(xf, _row(norm_ffn1[0]), ffn1_wg[0].astype(BF16), ffn1_wu[0].astype(BF16),
      ffn1_wd[0].astype(BF16), _row(norm_mix[0]), w_in[0].astype(BF16))

    wqk = jnp.concatenate([_blockdiag_dense(m_wq[0]), _blockdiag_dense(m_wk[0])], axis=-1).astype(BF16)
    wv = _blockdiag_dense(m_wv[0]).astype(BF16)
    wg3 = m_w_gates[0].reshape(3, D_MLSTM, N_GATES)
    wgc = jnp.pad(wg3, ((0, 0), (0, 0), (0, GATE_LANES - N_GATES))).astype(BF16)
    bgc = jnp.pad(m_b_gates[0].astype(F32), (0, GATE_LANES - N_GATES)).reshape(1, GATE_LANES)
    wax = jnp.concatenate([r_w_a[0], r_w_x[0]], axis=-1).astype(BF16)

    last = n_tiles - 1
    tile_spec = lambda j: pl.BlockSpec((TILE, D_MODEL), lambda i, j=j: (jnp.minimum(i, last), j))
    weights = [
        _permuted_causal(),
        m_conv_w[0].astype(F32), _row(m_conv_b[0]), wqk, wv, wgc, bgc,
        _row(m_ln_w[0]), _row(m_skip[0]),
        r_conv_w[0].astype(F32), _row(r_conv_b[0]), wax, _row(r_b_a[0]), _row(r_b_x[0]), _row(r_lam[0]),
        _row(out_norm_m[0]), _row(out_norm_r[0]), w_out[0].astype(BF16),
        _row(norm_ffn2[0]), ffn2_wg[0].astype(BF16), ffn2_wu[0].astype(BF16), ffn2_wd[0].astype(BF16),
        _row(norm_final),
    ]
    tile_f32 = pltpu.VMEM((TILE, D_MODEL), F32)
    out = pl.pallas_call(
        functools.partial(_mix_kernel, tiles_per_seq=seq // TILE),
        grid=(n_tiles + 1,),
        in_specs=[tile_spec(0), tile_spec(1), tile_spec(2), tile_spec(3), tile_spec(0)]
                 + [_resident(w.shape) for w in weights],
        out_specs=pl.BlockSpec((TILE, D_MODEL), lambda i: (jnp.maximum(i - 1, 0), 0)),
        out_shape=jax.ShapeDtypeStruct((n_tok, D_MODEL), F32),
        scratch_shapes=[
            pltpu.VMEM((WRAP_ROWS, D_MLSTM), F32),
            pltpu.VMEM((WRAP_ROWS, D_RGLRU), F32),
            pltpu.VMEM((HEADS, HEAD_DIM, HEAD_DIM), F32),
            pltpu.VMEM((HEADS, HEAD_DIM), F32),
            pltpu.VMEM((HEADS, GATE_LANES), F32),
            pltpu.VMEM((SUBLANES, D_RGLRU), F32),
            pltpu.VMEM((2, TILE, D_MODEL), F32),
            pltpu.VMEM((TILE, D_MODEL), BF16),
            pltpu.VMEM((2, TILE, D_FF), F32),
            pltpu.VMEM((TILE, D_FF), BF16),
            tile_f32, tile_f32, tile_f32,
            tile_f32,
            tile_f32,
            pltpu.VMEM((TILE, 2 * D_RGLRU), F32),
            tile_f32, tile_f32,
        ],
        compiler_params=params,
        name="mixers_ffn2",
    )(proj, proj, proj, proj, x1, *weights)
    return out.reshape(bsz, seq, D_MODEL)
```

```python
import functools

import jax
import jax.numpy as jnp
from jax import lax
from jax.experimental import pallas as pl
from jax.experimental.pallas import tpu as pltpu

D_MODEL = 1024
D_FF = 2816
D_MLSTM = 1024
D_RGLRU = 1024
HEADS = 4
HEAD_DIM = 256
QKV_BLOCK = 4
CONV_W = 4
RGLRU_C = 8.0
EPS = 1e-6
M_INIT = -1e30
N_GATES = 2 * HEADS

SUBLANES = 8
GATE_LANES = 128
TILE = 256
FFN1_TILES_PER_STEP = 2
SEG = TILE // SUBLANES
WRAP_ROWS = (CONV_W - 1) * SUBLANES
FF_CHUNK = 256
N_FF_CHUNKS = D_FF // FF_CHUNK
VMEM_LIMIT = 60 * 1024 * 1024

F32 = jnp.float32
BF16 = jnp.bfloat16


def _time_permute(x):
    t, d = x.shape
    return x.reshape(SUBLANES, t // SUBLANES, d).swapaxes(0, 1).reshape(t, d)


def _time_unpermute(x):
    t, d = x.shape
    return x.reshape(t // SUBLANES, SUBLANES, d).swapaxes(0, 1).reshape(t, d)


def _rms(x, g):
    return x * lax.rsqrt(jnp.mean(x * x, axis=-1, keepdims=True) + EPS) * g


def _dot(a, b):
    return jnp.dot(a, b, preferred_element_type=F32)


def _col_block(i):
    return slice(i * HEAD_DIM, (i + 1) * HEAD_DIM)


def _ffn_in_kernel(x_ref, g1_ref, wg_ref, wu_ref, wd_ref, g2_ref, win_ref, x1_ref, proj_ref):
    for sub in range(FFN1_TILES_PER_STEP):
        rows = slice(sub * TILE, (sub + 1) * TILE)
        x = _time_permute(x_ref[rows, :])
        xn = _rms(x, g1_ref[...]).astype(BF16)
        h = (jax.nn.silu(_dot(xn, wg_ref[...])) * _dot(xn, wu_ref[...])).astype(BF16)
        y = x + 0.5 * _dot(h, wd_ref[...])
        x1_ref[rows, :] = y
        proj_ref[rows, :] = _dot(_rms(y, g2_ref[...]).astype(BF16), win_ref[...])


def _ffn_gate_up(chunks, xn_ref, gu_ref, wg_ref, wu_ref):
    for c in chunks:
        sl = slice(c * FF_CHUNK, (c + 1) * FF_CHUNK)
        xn = xn_ref[...]
        gu_ref[0, :, sl] = _dot(xn, wg_ref[:, sl])
        gu_ref[1, :, sl] = _dot(xn, wu_ref[:, sl])


def _ffn_act(chunks, gu_ref, h_ref):
    for c in chunks:
        sl = slice(c * FF_CHUNK, (c + 1) * FF_CHUNK)
        h_ref[:, sl] = (jax.nn.silu(gu_ref[0, :, sl]) * gu_ref[1, :, sl]).astype(BF16)


def _causal_conv(tail_ref, x, w, bias, sl):
    t = x.shape[0]
    sub = lax.broadcasted_iota(jnp.int32, (SUBLANES, 1), 0)
    wrap = []
    for j in range(CONV_W - 1):
        cur = x[t - WRAP_ROWS + j * SUBLANES:t - WRAP_ROWS + (j + 1) * SUBLANES, :]
        prev = tail_ref[j * SUBLANES:(j + 1) * SUBLANES, sl]
        wrap.append(pltpu.roll(jnp.where(sub == SUBLANES - 1, prev, cur), 1, 0))
    tail_ref[:, sl] = x[t - WRAP_ROWS:t, :]
    ext = jnp.concatenate(wrap + [x], axis=0)
    y = bias
    for tap in range(CONV_W - 1):
        lo = WRAP_ROWS - (CONV_W - 1 - tap) * SUBLANES
        y = y + w[tap:tap + 1, :] * ext[lo:lo + t, :]
    return y + w[CONV_W - 1:CONV_W, :] * x


def _linear_scan(a, u, state_ref, sl):
    t = a.shape[0]
    hs, ps = [], []
    h = p = None
    for r in range(t // SUBLANES):
        a_r = a[r * SUBLANES:(r + 1) * SUBLANES, :]
        u_r = u[r * SUBLANES:(r + 1) * SUBLANES, :]
        h = u_r if h is None else a_r * h + u_r
        p = a_r if p is None else a_r * p
        hs.append(h)
        ps.append(p)
    sub = lax.broadcasted_iota(jnp.int32, (SUBLANES, 1), 0)
    a_seg, u_seg = p, h
    d = 1
    while d < SUBLANES:
        a_sh = jnp.where(sub < d, 1.0, pltpu.roll(a_seg, d, 0))
        u_sh = jnp.where(sub < d, 0.0, pltpu.roll(u_seg, d, 0))
        u_seg = a_seg * u_sh + u_seg
        a_seg = a_seg * a_sh
        d *= 2
    carry_in = state_ref[0:1, sl]
    seg_end = u_seg + a_seg * carry_in
    state_ref[0:1, sl] = seg_end[SUBLANES - 1:SUBLANES, :]
    seg_start = jnp.where(sub == 0, carry_in, pltpu.roll(seg_end, 1, 0))
    return jnp.concatenate([h_r + p_r * seg_start for h_r, p_r in zip(hs, ps)], axis=0)


def _cumulative_log_forget(tri, gcol):
    lf = jax.nn.log_sigmoid(gcol)
    hi = lf.astype(BF16)
    rest = lf - hi.astype(F32)
    mid = rest.astype(BF16)
    lo = (rest - mid.astype(F32)).astype(BF16)
    tri_b = tri.astype(BF16)
    return _dot(tri_b, hi) + _dot(tri_b, mid) + _dot(tri_b, lo)


def _mlstm_head(h, q, k, v, xc_h, z_h, gcol, bcol, grow, causal,
                ct_ref, n_ref, m_ref, lnw, skip):
    t = q.shape[0]
    qb = q.astype(BF16)
    ks = k * (HEAD_DIM ** -0.5)
    kb = ks.astype(BF16)
    b_i = bcol[:, HEADS + h:HEADS + h + 1]
    ig_i = gcol[:, h:h + 1]
    b_j = grow[HEADS + h:HEADS + h + 1, :]
    ig_j = grow[h:h + 1, :]
    m_prev = m_ref[h:h + 1, 0:1]

    dlog = jnp.where(causal, b_i - b_j + ig_j, -jnp.inf)
    inter = b_i + m_prev
    m_row = jnp.maximum(inter, jnp.max(dlog, axis=-1, keepdims=True))
    s = lax.dot_general(qb, kb, (((1,), (1,)), ((), ())), preferred_element_type=F32)
    s = s * jnp.exp(dlog - m_row)
    inter_w = jnp.exp(inter - m_row)
    num = _dot(s.astype(BF16), v.astype(BF16)) + inter_w * _dot(qb, ct_ref[h].astype(BF16))
    den = (jnp.sum(s, axis=-1, keepdims=True)
           + inter_w * jnp.sum(q * n_ref[h:h + 1, :], axis=-1, keepdims=True))
    hh = num / jnp.maximum(jnp.abs(den), jnp.exp(-m_row))

    g = b_i[t - 1:t, :]
    w = g - b_i + ig_i
    m_new = jnp.maximum(g + m_prev, jnp.max(w, axis=0, keepdims=True))
    decay = jnp.exp(g + m_prev - m_new)
    wk = jnp.exp(w - m_new)
    ct_ref[h] = decay * ct_ref[h] + lax.dot_general(
        kb, (v * wk).astype(BF16), (((0,), (0,)), ((), ())), preferred_element_type=F32)
    n_ref[h:h + 1, :] = decay * n_ref[h:h + 1, :] + jnp.sum(ks * wk, axis=0, keepdims=True)
    m_ref[h:h + 1, :] = jnp.broadcast_to(m_new, (1, m_ref.shape[1]))

    mu = jnp.mean(hh, axis=-1, keepdims=True)
    var = jnp.mean(jnp.square(hh - mu), axis=-1, keepdims=True)
    hn = (hh - mu) * lax.rsqrt(var + EPS) * lnw
    return (hn + skip * xc_h) * jax.nn.silu(z_h)


def _mix_kernel(xm_ref, zm_ref, xr_ref, yr_ref, x1_ref, tri_ref,
                mcw_ref, mcb_ref, wqk_ref, wv_ref, wgc_ref, bgc_ref,
                lnw_ref, skip_ref,
                rcw_ref, rcb_ref, wax_ref, ba_ref, bx_ref, lam_ref,
                onm_ref, onr_ref, wout_ref,
                g1_ref, wg_ref, wu_ref, wd_ref, g2_ref,
                o_ref,
                xm_tail, xr_tail, ct_ref, n_ref, m_ref, hr_ref, x2_ref,
                xn_ref, gu_ref, h_ref, q_ref, k_ref, v_ref, xc_ref, xrc_ref, pre_ref, outm_ref, outr_ref,
                *, tiles_per_seq):
    t = xm_ref.shape[0]
    step = pl.program_id(0)
    prev = (step + 1) % 2

    @pl.when(step == 0)
    def _init_handoff():
        x2_ref[...] = jnp.zeros_like(x2_ref)

    @pl.when(step % tiles_per_seq == 0)
    def _reset_state():
        xm_tail[...] = jnp.zeros_like(xm_tail)
        xr_tail[...] = jnp.zeros_like(xr_tail)
        ct_ref[...] = jnp.zeros_like(ct_ref)
        n_ref[...] = jnp.zeros_like(n_ref)
        m_ref[...] = jnp.full_like(m_ref, M_INIT)
        hr_ref[...] = jnp.zeros_like(hr_ref)

    xn_ref[...] = _rms(x2_ref[prev], g1_ref[...]).astype(BF16)
    gcol = jnp.broadcast_to(bgc_ref[...], (t, GATE_LANES))
    for h in range(HEADS):
        if h:
            _ffn_gate_up((h - 1,), xn_ref, gu_ref, wg_ref, wu_ref)
        sl = _col_block(h)
        xm = xm_ref[:, sl]
        xc = jax.nn.silu(_causal_conv(xm_tail, xm, mcw_ref[:, sl], mcb_ref[:, sl], sl))
        qk = _dot(xc.astype(BF16), wqk_ref[h])
        q, k = qk[:, :HEAD_DIM], qk[:, HEAD_DIM:]
        v = _dot(xm.astype(BF16), wv_ref[h])
        for part, val in enumerate((q, k, v)):
            gcol = gcol + _dot(val.astype(BF16), wgc_ref[part, sl, :])
        xc_ref[:, sl] = xc
        q_ref[:, sl] = q
        k_ref[:, sl] = k
        v_ref[:, sl] = v
    for g in range(HEADS):
        _ffn_gate_up((HEADS - 1 + g,), xn_ref, gu_ref, wg_ref, wu_ref)
        sl = _col_block(g)
        xrc = _causal_conv(xr_tail, xr_ref[:, sl], rcw_ref[:, sl], rcb_ref[:, sl], sl)
        xrc_ref[:, sl] = xrc
        pre_ref[:, 2 * g * HEAD_DIM:(2 * g + 2) * HEAD_DIM] = _dot(xrc.astype(BF16), wax_ref[g])

    tri = tri_ref[...]
    causal = tri > 0.0
    bcol = _cumulative_log_forget(tri, gcol)
    lane = lax.broadcasted_iota(jnp.int32, (1, GATE_LANES), 1)
    grow = jnp.where(lane < HEADS, gcol, bcol).T
    _ffn_gate_up(range(2 * HEADS - 1, N_FF_CHUNKS), xn_ref, gu_ref, wg_ref, wu_ref)
    for h in range(HEADS):
        sl = _col_block(h)
        outm_ref[:, sl] = _mlstm_head(h, q_ref[:, sl], k_ref[:, sl], v_ref[:, sl], xc_ref[:, sl], zm_ref[:, sl],
                                      gcol, bcol, grow, causal, ct_ref, n_ref, m_ref,
                                      lnw_ref[:, sl], skip_ref[:, sl])
        _ffn_act(((0, 1, 2, 3), (4, 5, 6), (7, 8), (9, 10))[h], gu_ref, h_ref)

    downs = []
    for g in range(HEADS):
        sl = _col_block(g)
        downs.append(_dot(h_ref[...], wd_ref[:, sl]))
        xrc = xrc_ref[:, sl]
        r = jax.nn.sigmoid(pre_ref[:, 2 * g * HEAD_DIM:(2 * g + 1) * HEAD_DIM] + ba_ref[:, sl])
        i = jax.nn.sigmoid(pre_ref[:, (2 * g + 1) * HEAD_DIM:(2 * g + 2) * HEAD_DIM] + bx_ref[:, sl])
        log_a = RGLRU_C * r * jax.nn.log_sigmoid(lam_ref[:, sl])
        a = jnp.exp(log_a)
        z = -jnp.tanh(log_a) * (1.0 + a * a)
        u = jnp.where(z > 0.0, z * lax.rsqrt(z), 0.0) * (i * xrc)
        hseq = _linear_scan(a, u, hr_ref, sl)
        outr_ref[:, sl] = hseq * jax.nn.gelu(yr_ref[:, sl])

    o_ref[...] = _time_unpermute(_rms(x2_ref[prev] + 0.5 * jnp.concatenate(downs, axis=1), g2_ref[...]))
    nm = _rms(outm_ref[...], onm_ref[...]).astype(BF16)
    nr = _rms(outr_ref[...], onr_ref[...]).astype(BF16)
    x2_ref[step % 2] = (x1_ref[...] + _dot(nm, wout_ref[0:D_MLSTM, :])
                        + _dot(nr, wout_ref[D_MLSTM:D_MLSTM + D_RGLRU, :]))


def _resident(shape):
    nd = len(shape)
    return pl.BlockSpec(shape, lambda *_: (0,) * nd, pipeline_mode=pl.Buffered(1))


def _blockdiag_dense(w):
    rows = w.reshape(HEADS, HEAD_DIM, QKV_BLOCK).astype(F32)
    col = jnp.arange(HEAD_DIM)
    spread = (col[None, :] % QKV_BLOCK == jnp.arange(QKV_BLOCK)[:, None]).astype(F32)
    dense = jnp.einsum('hro,oc->hrc', rows, spread, precision=lax.Precision.HIGHEST)
    same_block = col[:, None] // QKV_BLOCK == col[None, :] // QKV_BLOCK
    return jnp.where(same_block[None], dense, 0.0)


def _row(v):
    return v.reshape(1, -1).astype(F32)


def _permuted_causal():
    rho = jnp.arange(TILE)
    time = (rho % SUBLANES) * SEG + rho // SUBLANES
    return (time[None, :] <= time[:, None]).astype(F32)


def kernel(x, norm_ffn1, ffn1_wg, ffn1_wu, ffn1_wd, norm_mix, w_in, m_conv_w, m_conv_b, m_wq, m_wk, m_wv, m_w_gates, m_b_gates, m_ln_w, m_skip, r_conv_w, r_conv_b, r_w_a, r_b_a, r_w_x, r_b_x, r_lam, out_norm_m, out_norm_r, w_out, norm_ffn2, ffn2_wg, ffn2_wu, ffn2_wd, norm_final):
    bsz, seq, _ = x.shape
    n_tok = bsz * seq
    assert norm_ffn1.shape[0] == 1, "single-layer block"
    assert seq % TILE == 0
    d_proj = w_in.shape[-1]
    n_tiles = n_tok // TILE
    xf = x.reshape(n_tok, D_MODEL)
    params = pltpu.CompilerParams(dimension_semantics=("arbitrary",), vmem_limit_bytes=VMEM_LIMIT)

    assert n_tiles % FFN1_TILES_PER_STEP == 0
    step_rows = FFN1_TILES_PER_STEP * TILE
    x1, proj = pl.pallas_call(
        _ffn_in_kernel,
        grid=(n_tiles // FFN1_TILES_PER_STEP,),
        in_specs=[
            pl.BlockSpec((step_rows, D_MODEL), lambda i: (i, 0)),
            _resident((1, D_MODEL)),
            _resident((D_MODEL, D_FF)),
            _resident((D_MODEL, D_FF)),
            _resident((D_FF, D_MODEL)),
            _resident((1, D_MODEL)),
            _resident((D_MODEL, d_proj)),
        ],
        out_specs=[pl.BlockSpec((step_rows, D_MODEL), lambda i: (i, 0)),
                   pl.BlockSpec((step_rows, d_proj), lambda i: (i, 0))],
        out_shape=[jax.ShapeDtypeStruct((n_tok, D_MODEL), F32),
                   jax.ShapeDtypeStruct((n_tok, d_proj), F32)],
        compiler_params=params,
        name="ffn1_inproj",
    )(xf, _row(norm_ffn1[0]), ffn1_wg[0].astype(BF16), ffn1_wu[0].astype(BF16),
      ffn1_wd[0].astype(BF16), _row(norm_mix[0]), w_in[0].astype(BF16))

    wqk = jnp.concatenate([_blockdiag_dense(m_wq[0]), _blockdiag_dense(m_wk[0])], axis=-1).astype(BF16)
    wv = _blockdiag_dense(m_wv[0]).astype(BF16)
    wg3 = m_w_gates[0].reshape(3, D_MLSTM, N_GATES)
    wgc = jnp.pad(wg3, ((0, 0), (0, 0), (0, GATE_LANES - N_GATES))).astype(BF16)
    bgc = jnp.pad(m_b_gates[0].astype(F32), (0, GATE_LANES - N_GATES)).reshape(1, GATE_LANES)
    wax = jnp.concatenate([r_w_a[0], r_w_x[0]], axis=-1).astype(BF16)

    last = n_tiles - 1
    tile_spec = lambda j: pl.BlockSpec((TILE, D_MODEL), lambda i, j=j: (jnp.minimum(i, last), j))
    weights = [
        _permuted_causal(),
        m_conv_w[0].astype(F32), _row(m_conv_b[0]), wqk, wv, wgc, bgc,
        _row(m_ln_w[0]), _row(m_skip[0]),
        r_conv_w[0].astype(F32), _row(r_conv_b[0]), wax, _row(r_b_a[0]), _row(r_b_x[0]), _row(r_lam[0]),
        _row(out_norm_m[0]), _row(out_norm_r[0]), w_out[0].astype(BF16),
        _row(norm_ffn2[0]), ffn2_wg[0].astype(BF16), ffn2_wu[0].astype(BF16), ffn2_wd[0].astype(BF16),
        _row(norm_final),
    ]
    tile_f32 = pltpu.VMEM((TILE, D_MODEL), F32)
    out = pl.pallas_call(
        functools.partial(_mix_kernel, tiles_per_seq=seq // TILE),
        grid=(n_tiles + 1,),
        in_specs=[tile_spec(0), tile_spec(1), tile_spec(2), tile_spec(3), tile_spec(0)]
                 + [_resident(w.shape) for w in weights],
        out_specs=pl.BlockSpec((TILE, D_MODEL), lambda i: (jnp.maximum(i - 1, 0), 0)),
        out_shape=jax.ShapeDtypeStruct((n_tok, D_MODEL), F32),
        scratch_shapes=[
            pltpu.VMEM((WRAP_ROWS, D_MLSTM), F32),
            pltpu.VMEM((WRAP_ROWS, D_RGLRU), F32),
            pltpu.VMEM((HEADS, HEAD_DIM, HEAD_DIM), F32),
            pltpu.VMEM((HEADS, HEAD_DIM), F32),
            pltpu.VMEM((HEADS, GATE_LANES), F32),
            pltpu.VMEM((SUBLANES, D_RGLRU), F32),
            pltpu.VMEM((2, TILE, D_MODEL), F32),
            pltpu.VMEM((TILE, D_MODEL), BF16),
            pltpu.VMEM((2, TILE, D_FF), F32),
            pltpu.VMEM((TILE, D_FF), BF16),
            tile_f32, tile_f32, tile_f32,
            tile_f32,
            tile_f32,
            pltpu.VMEM((TILE, 2 * D_RGLRU), F32),
            tile_f32, tile_f32,
        ],
        compiler_params=params,
        name="mixers_ffn2",
    )(proj, proj, proj, proj, x1, *weights)
    return out.reshape(bsz, seq, D_MODEL)
```

```python
import functools

import jax
import jax.numpy as jnp
from jax import lax
from jax.experimental import pallas as pl
from jax.experimental.pallas import tpu as pltpu

D_MODEL = 1024
D_FF = 2816
D_MLSTM = 1024
D_RGLRU = 1024
HEADS = 4
HEAD_DIM = 256
QKV_BLOCK = 4
CONV_W = 4
RGLRU_C = 8.0
EPS = 1e-6
M_INIT = -1e30
N_GATES = 2 * HEADS

SUBLANES = 8
GATE_LANES = 128
TILE = 256
SEG = TILE // SUBLANES
WRAP_ROWS = (CONV_W - 1) * SUBLANES
FF_CHUNK = 256
N_FF_CHUNKS = D_FF // FF_CHUNK
SETUP_STEPS = 16
VMEM_LIMIT = 60 * 1024 * 1024

F32 = jnp.float32
BF16 = jnp.bfloat16


def _time_permute(x):
    t, d = x.shape
    return x.reshape(SUBLANES, t // SUBLANES, d).swapaxes(0, 1).reshape(t, d)


def _time_unpermute(x):
    t, d = x.shape
    return x.reshape(t // SUBLANES, SUBLANES, d).swapaxes(0, 1).reshape(t, d)


def _rms(x, g):
    return x * lax.rsqrt(jnp.mean(x * x, axis=-1, keepdims=True) + EPS) * g


def _dot(a, b):
    return jnp.dot(a, b, preferred_element_type=F32)


def _cast_weight_chunk(step, src_ref, dst_ref):
    n = src_ref.shape[0]
    dst_ref[pl.ds(pl.multiple_of(step * n, n), n), :] = src_ref[...].astype(BF16)


def _col_block(i):
    return slice(i * HEAD_DIM, (i + 1) * HEAD_DIM)


def _ffn_in_kernel(x_ref, g1_ref, wg32_ref, wu32_ref, wd32_ref, g2_ref, win32_ref, x1_ref, proj_ref,
                   wg_ref, wu_ref, wd_ref, win_ref):
    step = pl.program_id(0)

    @pl.when(step < SETUP_STEPS)
    def _load_weights():
        for src, dst in ((wg32_ref, wg_ref), (wu32_ref, wu_ref), (wd32_ref, wd_ref), (win32_ref, win_ref)):
            _cast_weight_chunk(step, src, dst)

    @pl.when(step >= SETUP_STEPS)
    def _tile():
        x = _time_permute(x_ref[...])
        xn = _rms(x, g1_ref[...]).astype(BF16)
        h = (jax.nn.silu(_dot(xn, wg_ref[...])) * _dot(xn, wu_ref[...])).astype(BF16)
        y = x + 0.5 * _dot(h, wd_ref[...])
        x1_ref[...] = y
        proj_ref[...] = _dot(_rms(y, g2_ref[...]).astype(BF16), win_ref[...])


def _ffn_gate_up(chunks, xn_ref, gu_ref, wg_ref, wu_ref):
    for c in chunks:
        sl = slice(c * FF_CHUNK, (c + 1) * FF_CHUNK)
        xn = xn_ref[...]
        gu_ref[0, :, sl] = _dot(xn, wg_ref[:, sl])
        gu_ref[1, :, sl] = _dot(xn, wu_ref[:, sl])


def _ffn_act(chunks, gu_ref, h_ref):
    for c in chunks:
        sl = slice(c * FF_CHUNK, (c + 1) * FF_CHUNK)
        h_ref[:, sl] = (jax.nn.silu(gu_ref[0, :, sl]) * gu_ref[1, :, sl]).astype(BF16)


def _causal_conv(tail_ref, x, w, bias, sl):
    t = x.shape[0]
    sub = lax.broadcasted_iota(jnp.int32, (SUBLANES, 1), 0)
    wrap = []
    for j in range(CONV_W - 1):
        cur = x[t - WRAP_ROWS + j * SUBLANES:t - WRAP_ROWS + (j + 1) * SUBLANES, :]
        prev = tail_ref[j * SUBLANES:(j + 1) * SUBLANES, sl]
        wrap.append(pltpu.roll(jnp.where(sub == SUBLANES - 1, prev, cur), 1, 0))
    tail_ref[:, sl] = x[t - WRAP_ROWS:t, :]
    ext = jnp.concatenate(wrap + [x], axis=0)
    y = bias
    for tap in range(CONV_W - 1):
        lo = WRAP_ROWS - (CONV_W - 1 - tap) * SUBLANES
        y = y + w[tap:tap + 1, :] * ext[lo:lo + t, :]
    return y + w[CONV_W - 1:CONV_W, :] * x


def _linear_scan(a, u, state_ref, sl):
    t = a.shape[0]
    hs, ps = [], []
    h = p = None
    for r in range(t // SUBLANES):
        a_r = a[r * SUBLANES:(r + 1) * SUBLANES, :]
        u_r = u[r * SUBLANES:(r + 1) * SUBLANES, :]
        h = u_r if h is None else a_r * h + u_r
        p = a_r if p is None else a_r * p
        hs.append(h)
        ps.append(p)
    sub = lax.broadcasted_iota(jnp.int32, (SUBLANES, 1), 0)
    a_seg, u_seg = p, h
    d = 1
    while d < SUBLANES:
        a_sh = jnp.where(sub < d, 1.0, pltpu.roll(a_seg, d, 0))
        u_sh = jnp.where(sub < d, 0.0, pltpu.roll(u_seg, d, 0))
        u_seg = a_seg * u_sh + u_seg
        a_seg = a_seg * a_sh
        d *= 2
    carry_in = state_ref[0:1, sl]
    seg_end = u_seg + a_seg * carry_in
    state_ref[0:1, sl] = seg_end[SUBLANES - 1:SUBLANES, :]
    seg_start = jnp.where(sub == 0, carry_in, pltpu.roll(seg_end, 1, 0))
    return jnp.concatenate([h_r + p_r * seg_start for h_r, p_r in zip(hs, ps)], axis=0)


def _cumulative_log_forget(tri, gcol):
    lf = jax.nn.log_sigmoid(gcol)
    hi = lf.astype(BF16)
    rest = lf - hi.astype(F32)
    mid = rest.astype(BF16)
    lo = (rest - mid.astype(F32)).astype(BF16)
    tri_b = tri.astype(BF16)
    return _dot(tri_b, hi) + _dot(tri_b, mid) + _dot(tri_b, lo)


def _mlstm_head(h, q, k, v, xc_h, z_h, gcol, bcol, grow, causal,
                ct_ref, n_ref, m_ref, lnw, skip):
    t = q.shape[0]
    qb = q.astype(BF16)
    ks = k * (HEAD_DIM ** -0.5)
    kb = ks.astype(BF16)
    b_i = bcol[:, HEADS + h:HEADS + h + 1]
    ig_i = gcol[:, h:h + 1]
    b_j = grow[HEADS + h:HEADS + h + 1, :]
    ig_j = grow[h:h + 1, :]
    m_prev = m_ref[h:h + 1, 0:1]

    dlog = jnp.where(causal, b_i - b_j + ig_j, -jnp.inf)
    inter = b_i + m_prev
    m_row = jnp.maximum(inter, jnp.max(dlog, axis=-1, keepdims=True))
    s = lax.dot_general(qb, kb, (((1,), (1,)), ((), ())), preferred_element_type=F32)
    s = s * jnp.exp(dlog - m_row)
    inter_w = jnp.exp(inter - m_row)
    num = _dot(s.astype(BF16), v.astype(BF16)) + inter_w * _dot(qb, ct_ref[h].astype(BF16))
    den = (jnp.sum(s, axis=-1, keepdims=True)
           + inter_w * jnp.sum(q * n_ref[h:h + 1, :], axis=-1, keepdims=True))
    hh = num / jnp.maximum(jnp.abs(den), jnp.exp(-m_row))

    g = b_i[t - 1:t, :]
    w = g - b_i + ig_i
    m_new = jnp.maximum(g + m_prev, jnp.max(w, axis=0, keepdims=True))
    decay = jnp.exp(g + m_prev - m_new)
    wk = jnp.exp(w - m_new)
    ct_ref[h] = decay * ct_ref[h] + lax.dot_general(
        kb, (v * wk).astype(BF16), (((0,), (0,)), ((), ())), preferred_element_type=F32)
    n_ref[h:h + 1, :] = decay * n_ref[h:h + 1, :] + jnp.sum(ks * wk, axis=0, keepdims=True)
    m_ref[h:h + 1, :] = jnp.broadcast_to(m_new, (1, m_ref.shape[1]))

    mu = jnp.mean(hh, axis=-1, keepdims=True)
    var = jnp.mean(jnp.square(hh - mu), axis=-1, keepdims=True)
    hn = (hh - mu) * lax.rsqrt(var + EPS) * lnw
    return (hn + skip * xc_h) * jax.nn.silu(z_h)


def _mix_kernel(xm_ref, zm_ref, xr_ref, yr_ref, x1_ref, tri_ref,
                mcw_ref, mcb_ref, wqk_ref, wv_ref, wgc_ref, bgc_ref,
                lnw_ref, skip_ref,
                rcw_ref, rcb_ref, wax_ref, ba_ref, bx_ref, lam_ref,
                onm_ref, onr_ref,
                g1_ref, wg32_ref, wu32_ref, wd32_ref, wout32_ref, g2_ref,
                o_ref,
                xm_tail, xr_tail, ct_ref, n_ref, m_ref, hr_ref, x2_ref,
                xn_ref, gu_ref, h_ref, q_ref, k_ref, v_ref, xc_ref, xrc_ref, pre_ref, outm_ref, outr_ref,
                wg_ref, wu_ref, wd_ref, wout_ref,
                *, tiles_per_seq):
    grid_step = pl.program_id(0)

    @pl.when(grid_step < SETUP_STEPS)
    def _load_weights():
        for src, dst in ((wg32_ref, wg_ref), (wu32_ref, wu_ref), (wd32_ref, wd_ref), (wout32_ref, wout_ref)):
            _cast_weight_chunk(grid_step, src, dst)

    @pl.when(grid_step >= SETUP_STEPS)
    def _tile():
        t = xm_ref.shape[0]
        step = pl.program_id(0) - SETUP_STEPS
        prev = (step + 1) % 2

        @pl.when(step == 0)
        def _init_handoff():
            x2_ref[...] = jnp.zeros_like(x2_ref)

        @pl.when(step % tiles_per_seq == 0)
        def _reset_state():
            xm_tail[...] = jnp.zeros_like(xm_tail)
            xr_tail[...] = jnp.zeros_like(xr_tail)
            ct_ref[...] = jnp.zeros_like(ct_ref)
            n_ref[...] = jnp.zeros_like(n_ref)
            m_ref[...] = jnp.full_like(m_ref, M_INIT)
            hr_ref[...] = jnp.zeros_like(hr_ref)

        xn_ref[...] = _rms(x2_ref[prev], g1_ref[...]).astype(BF16)
        gcol = jnp.broadcast_to(bgc_ref[...], (t, GATE_LANES))
        for h in range(HEADS):
            if h:
                _ffn_gate_up((h - 1,), xn_ref, gu_ref, wg_ref, wu_ref)
            sl = _col_block(h)
            xm = xm_ref[:, sl]
            xc = jax.nn.silu(_causal_conv(xm_tail, xm, mcw_ref[:, sl], mcb_ref[:, sl], sl))
            qk = _dot(xc.astype(BF16), wqk_ref[h])
            q, k = qk[:, :HEAD_DIM], qk[:, HEAD_DIM:]
            v = _dot(xm.astype(BF16), wv_ref[h])
            for part, val in enumerate((q, k, v)):
                gcol = gcol + _dot(val.astype(BF16), wgc_ref[part, sl, :])
            xc_ref[:, sl] = xc
            q_ref[:, sl] = q
            k_ref[:, sl] = k
            v_ref[:, sl] = v
        for g in range(HEADS):
            _ffn_gate_up((HEADS - 1 + g,), xn_ref, gu_ref, wg_ref, wu_ref)
            sl = _col_block(g)
            xrc = _causal_conv(xr_tail, xr_ref[:, sl], rcw_ref[:, sl], rcb_ref[:, sl], sl)
            xrc_ref[:, sl] = xrc
            pre_ref[:, 2 * g * HEAD_DIM:(2 * g + 2) * HEAD_DIM] = _dot(xrc.astype(BF16), wax_ref[g])

        tri = tri_ref[...]
        causal = tri > 0.0
        bcol = _cumulative_log_forget(tri, gcol)
        lane = lax.broadcasted_iota(jnp.int32, (1, GATE_LANES), 1)
        grow = jnp.where(lane < HEADS, gcol, bcol).T
        _ffn_gate_up(range(2 * HEADS - 1, N_FF_CHUNKS), xn_ref, gu_ref, wg_ref, wu_ref)
        for h in range(HEADS):
            sl = _col_block(h)
            outm_ref[:, sl] = _mlstm_head(h, q_ref[:, sl], k_ref[:, sl], v_ref[:, sl], xc_ref[:, sl], zm_ref[:, sl],
                                          gcol, bcol, grow, causal, ct_ref, n_ref, m_ref,
                                          lnw_ref[:, sl], skip_ref[:, sl])
            _ffn_act(((0, 1, 2, 3), (4, 5, 6), (7, 8), (9, 10))[h], gu_ref, h_ref)

        downs = []
        for g in range(HEADS):
            sl = _col_block(g)
            downs.append(_dot(h_ref[...], wd_ref[:, sl]))
            xrc = xrc_ref[:, sl]
            r = jax.nn.sigmoid(pre_ref[:, 2 * g * HEAD_DIM:(2 * g + 1) * HEAD_DIM] + ba_ref[:, sl])
            i = jax.nn.sigmoid(pre_ref[:, (2 * g + 1) * HEAD_DIM:(2 * g + 2) * HEAD_DIM] + bx_ref[:, sl])
            log_a = RGLRU_C * r * jax.nn.log_sigmoid(lam_ref[:, sl])
            a = jnp.exp(log_a)
            z = -jnp.tanh(log_a) * (1.0 + a * a)
            u = jnp.where(z > 0.0, z * lax.rsqrt(z), 0.0) * (i * xrc)
            hseq = _linear_scan(a, u, hr_ref, sl)
            outr_ref[:, sl] = hseq * jax.nn.gelu(yr_ref[:, sl])

        o_ref[...] = _time_unpermute(_rms(x2_ref[prev] + 0.5 * jnp.concatenate(downs, axis=1), g2_ref[...]))
        nm = _rms(outm_ref[...], onm_ref[...]).astype(BF16)
        nr = _rms(outr_ref[...], onr_ref[...]).astype(BF16)
        x2_ref[step % 2] = (x1_ref[...] + _dot(nm, wout_ref[0:D_MLSTM, :])
                            + _dot(nr, wout_ref[D_MLSTM:D_MLSTM + D_RGLRU, :]))


def _row_chunks(shape):
    rows, cols = shape
    return pl.BlockSpec((rows // SETUP_STEPS, cols), lambda i: (jnp.minimum(i, SETUP_STEPS - 1), 0))


def _resident(shape):
    nd = len(shape)
    return pl.BlockSpec(shape, lambda *_: (0,) * nd, pipeline_mode=pl.Buffered(1))


def _blockdiag_dense(w):
    rows = w.reshape(HEADS, HEAD_DIM, QKV_BLOCK).astype(F32)
    col = jnp.arange(HEAD_DIM)
    spread = (col[None, :] % QKV_BLOCK == jnp.arange(QKV_BLOCK)[:, None]).astype(F32)
    dense = jnp.einsum('hro,oc->hrc', rows, spread, precision=lax.Precision.HIGHEST)
    same_block = col[:, None] // QKV_BLOCK == col[None, :] // QKV_BLOCK
    return jnp.where(same_block[None], dense, 0.0)


def _row(v):
    return v.reshape(1, -1).astype(F32)


def _permuted_causal():
    rho = jnp.arange(TILE)
    time = (rho % SUBLANES) * SEG + rho // SUBLANES
    return (time[None, :] <= time[:, None]).astype(F32)


def kernel(x, norm_ffn1, ffn1_wg, ffn1_wu, ffn1_wd, norm_mix, w_in, m_conv_w, m_conv_b, m_wq, m_wk, m_wv, m_w_gates, m_b_gates, m_ln_w, m_skip, r_conv_w, r_conv_b, r_w_a, r_b_a, r_w_x, r_b_x, r_lam, out_norm_m, out_norm_r, w_out, norm_ffn2, ffn2_wg, ffn2_wu, ffn2_wd, norm_final):
    bsz, seq, _ = x.shape
    n_tok = bsz * seq
    assert norm_ffn1.shape[0] == 1, "single-layer block"
    assert seq % TILE == 0
    d_proj = w_in.shape[-1]
    n_tiles = n_tok // TILE
    xf = x.reshape(n_tok, D_MODEL)
    params = pltpu.CompilerParams(dimension_semantics=("arbitrary",), vmem_limit_bytes=VMEM_LIMIT)

    x1, proj = pl.pallas_call(
        _ffn_in_kernel,
        grid=(SETUP_STEPS + n_tiles,),
        in_specs=[
            pl.BlockSpec((TILE, D_MODEL), lambda i: (jnp.maximum(i - SETUP_STEPS, 0), 0)),
            _resident((1, D_MODEL)),
            _row_chunks((D_MODEL, D_FF)),
            _row_chunks((D_MODEL, D_FF)),
            _row_chunks((D_FF, D_MODEL)),
            _resident((1, D_MODEL)),
            _row_chunks((D_MODEL, d_proj)),
        ],
        out_specs=[pl.BlockSpec((TILE, D_MODEL), lambda i: (jnp.maximum(i - SETUP_STEPS, 0), 0)),
                   pl.BlockSpec((TILE, d_proj), lambda i: (jnp.maximum(i - SETUP_STEPS, 0), 0))],
        out_shape=[jax.ShapeDtypeStruct((n_tok, D_MODEL), F32),
                   jax.ShapeDtypeStruct((n_tok, d_proj), F32)],
        scratch_shapes=[pltpu.VMEM((D_MODEL, D_FF), BF16), pltpu.VMEM((D_MODEL, D_FF), BF16),
                        pltpu.VMEM((D_FF, D_MODEL), BF16), pltpu.VMEM((D_MODEL, d_proj), BF16)],
        compiler_params=params,
        name="ffn1_inproj",
    )(xf, _row(norm_ffn1[0]), ffn1_wg[0], ffn1_wu[0], ffn1_wd[0], _row(norm_mix[0]), w_in[0])

    wqk = jnp.concatenate([_blockdiag_dense(m_wq[0]), _blockdiag_dense(m_wk[0])], axis=-1).astype(BF16)
    wv = _blockdiag_dense(m_wv[0]).astype(BF16)
    wg3 = m_w_gates[0].reshape(3, D_MLSTM, N_GATES)
    wgc = jnp.pad(wg3, ((0, 0), (0, 0), (0, GATE_LANES - N_GATES))).astype(BF16)
    bgc = jnp.pad(m_b_gates[0].astype(F32), (0, GATE_LANES - N_GATES)).reshape(1, GATE_LANES)
    wax = jnp.concatenate([r_w_a[0], r_w_x[0]], axis=-1).astype(BF16)

    last = n_tiles - 1
    tile_spec = lambda j: pl.BlockSpec((TILE, D_MODEL), lambda i, j=j: (jnp.clip(i - SETUP_STEPS, 0, last), j))
    weights = [
        _permuted_causal(),
        m_conv_w[0].astype(F32), _row(m_conv_b[0]), wqk, wv, wgc, bgc,
        _row(m_ln_w[0]), _row(m_skip[0]),
        r_conv_w[0].astype(F32), _row(r_conv_b[0]), wax, _row(r_b_a[0]), _row(r_b_x[0]), _row(r_lam[0]),
        _row(out_norm_m[0]), _row(out_norm_r[0]),
        _row(norm_ffn2[0]),
    ]
    streamed = [ffn2_wg[0], ffn2_wu[0], ffn2_wd[0], w_out[0]]
    tile_f32 = pltpu.VMEM((TILE, D_MODEL), F32)
    out = pl.pallas_call(
        functools.partial(_mix_kernel, tiles_per_seq=seq // TILE),
        grid=(SETUP_STEPS + n_tiles + 1,),
        in_specs=[tile_spec(0), tile_spec(1), tile_spec(2), tile_spec(3), tile_spec(0)]
                 + [_resident(w.shape) for w in weights] + [_row_chunks(w.shape) for w in streamed]
                 + [_resident((1, D_MODEL))],
        out_specs=pl.BlockSpec((TILE, D_MODEL), lambda i: (jnp.maximum(i - SETUP_STEPS - 1, 0), 0)),
        out_shape=jax.ShapeDtypeStruct((n_tok, D_MODEL), F32),
        scratch_shapes=[
            pltpu.VMEM((WRAP_ROWS, D_MLSTM), F32),
            pltpu.VMEM((WRAP_ROWS, D_RGLRU), F32),
            pltpu.VMEM((HEADS, HEAD_DIM, HEAD_DIM), F32),
            pltpu.VMEM((HEADS, HEAD_DIM), F32),
            pltpu.VMEM((HEADS, GATE_LANES), F32),
            pltpu.VMEM((SUBLANES, D_RGLRU), F32),
            pltpu.VMEM((2, TILE, D_MODEL), F32),
            pltpu.VMEM((TILE, D_MODEL), BF16),
            pltpu.VMEM((2, TILE, D_FF), F32),
            pltpu.VMEM((TILE, D_FF), BF16),
            tile_f32, tile_f32, tile_f32,
            tile_f32,
            tile_f32,
            pltpu.VMEM((TILE, 2 * D_RGLRU), F32),
            tile_f32, tile_f32,
            pltpu.VMEM((D_MODEL, D_FF), BF16), pltpu.VMEM((D_MODEL, D_FF), BF16),
            pltpu.VMEM((D_FF, D_MODEL), BF16),
            pltpu.VMEM((D_MLSTM + D_RGLRU, D_MODEL), BF16),
        ],
        compiler_params=params,
        name="mixers_ffn2",
    )(proj, proj, proj, proj, x1, *weights, *streamed, _row(norm_final))
    return out.reshape(bsz, seq, D_MODEL)
```

```python
import functools

import jax
import jax.numpy as jnp
from jax import lax
from jax.experimental import pallas as pl
from jax.experimental.pallas import tpu as pltpu

D_MODEL = 1024
D_FF = 2816
D_MLSTM = 1024
D_RGLRU = 1024
HEADS = 4
HEAD_DIM = 256
QKV_BLOCK = 4
CONV_W = 4
RGLRU_C = 8.0
EPS = 1e-6
M_INIT = -1e30
N_GATES = 2 * HEADS

SUBLANES = 8
GATE_LANES = 128
TILE = 256
SEG = TILE // SUBLANES
WRAP_ROWS = (CONV_W - 1) * SUBLANES
FF_CHUNK = 256
N_FF_CHUNKS = D_FF // FF_CHUNK
SETUP_STEPS = 16
VMEM_LIMIT = 60 * 1024 * 1024

F32 = jnp.float32
BF16 = jnp.bfloat16


def _time_permute(x):
    t, d = x.shape
    return x.reshape(SUBLANES, t // SUBLANES, d).swapaxes(0, 1).reshape(t, d)


def _time_unpermute(x):
    t, d = x.shape
    return x.reshape(t // SUBLANES, SUBLANES, d).swapaxes(0, 1).reshape(t, d)


def _rms(x, g):
    return x * lax.rsqrt(jnp.mean(x * x, axis=-1, keepdims=True) + EPS) * g


def _dot(a, b):
    return jnp.dot(a, b, preferred_element_type=F32)


def _cast_weight_chunk(step, src_ref, dst_ref):
    n = src_ref.shape[0]
    dst_ref[pl.ds(pl.multiple_of(step * n, n), n), :] = src_ref[...].astype(BF16)


def _col_block(i):
    return slice(i * HEAD_DIM, (i + 1) * HEAD_DIM)


def _ffn_in_kernel(x_ref, g1_ref, wg32_ref, wu32_ref, wd32_ref, g2_ref, win32_ref, x1_ref, proj_ref,
                   wg_ref, wu_ref, wd_ref, win_ref):
    step = pl.program_id(0)

    @pl.when(step < SETUP_STEPS)
    def _load_weights():
        for src, dst in ((wg32_ref, wg_ref), (wu32_ref, wu_ref), (wd32_ref, wd_ref), (win32_ref, win_ref)):
            _cast_weight_chunk(step, src, dst)

    @pl.when(step >= SETUP_STEPS)
    def _tile():
        x = _time_permute(x_ref[...])
        xn = _rms(x, g1_ref[...]).astype(BF16)
        h = (jax.nn.silu(_dot(xn, wg_ref[...])) * _dot(xn, wu_ref[...])).astype(BF16)
        y = x + 0.5 * _dot(h, wd_ref[...])
        x1_ref[...] = y
        proj_ref[...] = _dot(_rms(y, g2_ref[...]).astype(BF16), win_ref[...])


def _ffn_gate_up(chunks, xn_ref, gu_ref, wg_ref, wu_ref):
    for c in chunks:
        sl = slice(c * FF_CHUNK, (c + 1) * FF_CHUNK)
        xn = xn_ref[...]
        gu_ref[0, :, sl] = _dot(xn, wg_ref[:, sl])
        gu_ref[1, :, sl] = _dot(xn, wu_ref[:, sl])


def _ffn_act(chunks, gu_ref, h_ref):
    for c in chunks:
        sl = slice(c * FF_CHUNK, (c + 1) * FF_CHUNK)
        h_ref[:, sl] = (jax.nn.silu(gu_ref[0, :, sl]) * gu_ref[1, :, sl]).astype(BF16)


def _causal_conv(tail_ref, x, w, bias, sl):
    t = x.shape[0]
    sub = lax.broadcasted_iota(jnp.int32, (SUBLANES, 1), 0)
    wrap = []
    for j in range(CONV_W - 1):
        cur = x[t - WRAP_ROWS + j * SUBLANES:t - WRAP_ROWS + (j + 1) * SUBLANES, :]
        prev = tail_ref[j * SUBLANES:(j + 1) * SUBLANES, sl]
        wrap.append(pltpu.roll(jnp.where(sub == SUBLANES - 1, prev, cur), 1, 0))
    tail_ref[:, sl] = x[t - WRAP_ROWS:t, :]
    ext = jnp.concatenate(wrap + [x], axis=0)
    y = bias
    for tap in range(CONV_W - 1):
        lo = WRAP_ROWS - (CONV_W - 1 - tap) * SUBLANES
        y = y + w[tap:tap + 1, :] * ext[lo:lo + t, :]
    return y + w[CONV_W - 1:CONV_W, :] * x


def _linear_scan(a, u, state_ref, sl):
    t = a.shape[0]
    hs, ps = [], []
    h = p = None
    for r in range(t // SUBLANES):
        a_r = a[r * SUBLANES:(r + 1) * SUBLANES, :]
        u_r = u[r * SUBLANES:(r + 1) * SUBLANES, :]
        h = u_r if h is None else a_r * h + u_r
        p = a_r if p is None else a_r * p
        hs.append(h)
        ps.append(p)
    sub = lax.broadcasted_iota(jnp.int32, (SUBLANES, 1), 0)
    a_seg, u_seg = p, h
    d = 1
    while d < SUBLANES:
        a_sh = jnp.where(sub < d, 1.0, pltpu.roll(a_seg, d, 0))
        u_sh = jnp.where(sub < d, 0.0, pltpu.roll(u_seg, d, 0))
        u_seg = a_seg * u_sh + u_seg
        a_seg = a_seg * a_sh
        d *= 2
    carry_in = state_ref[0:1, sl]
    seg_end = u_seg + a_seg * carry_in
    state_ref[0:1, sl] = seg_end[SUBLANES - 1:SUBLANES, :]
    seg_start = jnp.where(sub == 0, carry_in, pltpu.roll(seg_end, 1, 0))
    return jnp.concatenate([h_r + p_r * seg_start for h_r, p_r in zip(hs, ps)], axis=0)


def _cumulative_log_forget(tri, gcol):
    lf = jax.nn.log_sigmoid(gcol)
    hi = lf.astype(BF16)
    rest = lf - hi.astype(F32)
    mid = rest.astype(BF16)
    lo = (rest - mid.astype(F32)).astype(BF16)
    tri_b = tri.astype(BF16)
    return _dot(tri_b, hi) + _dot(tri_b, mid) + _dot(tri_b, lo)


def _mlstm_head(h, q, k, v, xc_h, z_h, gcol, bcol, grow, causal,
                ct_ref, n_ref, m_ref, lnw, skip):
    t = q.shape[0]
    qb = q.astype(BF16)
    ks = k * (HEAD_DIM ** -0.5)
    kb = ks.astype(BF16)
    b_i = bcol[:, HEADS + h:HEADS + h + 1]
    ig_i = gcol[:, h:h + 1]
    b_j = grow[HEADS + h:HEADS + h + 1, :]
    ig_j = grow[h:h + 1, :]
    m_prev = m_ref[h:h + 1, 0:1]

    dlog = jnp.where(causal, b_i - b_j + ig_j, -jnp.inf)
    inter = b_i + m_prev
    m_row = jnp.maximum(inter, jnp.max(dlog, axis=-1, keepdims=True))
    s = lax.dot_general(qb, kb, (((1,), (1,)), ((), ())), preferred_element_type=F32)
    s = s * jnp.exp(dlog - m_row)
    inter_w = jnp.exp(inter - m_row)
    num = _dot(s.astype(BF16), v.astype(BF16)) + inter_w * _dot(qb, ct_ref[h].astype(BF16))
    den = (jnp.sum(s, axis=-1, keepdims=True)
           + inter_w * jnp.sum(q * n_ref[h:h + 1, :], axis=-1, keepdims=True))
    hh = num / jnp.maximum(jnp.abs(den), jnp.exp(-m_row))

    g = b_i[t - 1:t, :]
    w = g - b_i + ig_i
    m_new = jnp.maximum(g + m_prev, jnp.max(w, axis=0, keepdims=True))
    decay = jnp.exp(g + m_prev - m_new)
    wk = jnp.exp(w - m_new)
    ct_ref[h] = decay * ct_ref[h] + lax.dot_general(
        kb, (v * wk).astype(BF16), (((0,), (0,)), ((), ())), preferred_element_type=F32)
    n_ref[h:h + 1, :] = decay * n_ref[h:h + 1, :] + jnp.sum(ks * wk, axis=0, keepdims=True)
    m_ref[h:h + 1, :] = jnp.broadcast_to(m_new, (1, m_ref.shape[1]))

    mu = jnp.mean(hh, axis=-1, keepdims=True)
    var = jnp.mean(jnp.square(hh - mu), axis=-1, keepdims=True)
    hn = (hh - mu) * lax.rsqrt(var + EPS) * lnw
    return (hn + skip * xc_h) * jax.nn.silu(z_h)


def _mix_kernel(xm_ref, zm_ref, xr_ref, yr_ref, x1_ref, tri_ref,
                mcw_ref, mcb_ref, wqk_ref, wv_ref, wgc_ref, bgc_ref,
                lnw_ref, skip_ref,
                rcw_ref, rcb_ref, wax_ref, ba_ref, bx_ref, lam_ref,
                onm_ref, onr_ref,
                g1_ref, wg32_ref, wu32_ref, wd32_ref, wout32_ref, g2_ref,
                o_ref,
                xm_tail, xr_tail, ct_ref, n_ref, m_ref, hr_ref, x2_ref,
                xn_ref, gu_ref, h_ref, q_ref, k_ref, v_ref, xc_ref, xrc_ref, pre_ref, outm_ref, outr_ref,
                wg_ref, wu_ref, wd_ref, wout_ref, gfold_ref,
                *, tiles_per_seq):
    grid_step = pl.program_id(0)

    @pl.when(grid_step < SETUP_STEPS)
    def _load_weights():
        for src, dst in ((wg32_ref, wg_ref), (wu32_ref, wu_ref), (wd32_ref, wd_ref), (wout32_ref, wout_ref)):
            _cast_weight_chunk(grid_step, src, dst)

        @pl.when(grid_step == 0)
        def _fold_gate_weights():
            for h in range(HEADS):
                sl = _col_block(h)
                wqk = wqk_ref[h]
                gfold_ref[0, sl, :] = (_dot(wqk[:, :HEAD_DIM], wgc_ref[0, sl, :])
                                       + _dot(wqk[:, HEAD_DIM:], wgc_ref[1, sl, :])).astype(BF16)
                gfold_ref[1, sl, :] = _dot(wv_ref[h], wgc_ref[2, sl, :]).astype(BF16)

    @pl.when(grid_step >= SETUP_STEPS)
    def _tile():
        t = xm_ref.shape[0]
        step = pl.program_id(0) - SETUP_STEPS
        prev = (step + 1) % 2

        @pl.when(step == 0)
        def _init_handoff():
            x2_ref[...] = jnp.zeros_like(x2_ref)

        @pl.when(step % tiles_per_seq == 0)
        def _reset_state():
            xm_tail[...] = jnp.zeros_like(xm_tail)
            xr_tail[...] = jnp.zeros_like(xr_tail)
            ct_ref[...] = jnp.zeros_like(ct_ref)
            n_ref[...] = jnp.zeros_like(n_ref)
            m_ref[...] = jnp.full_like(m_ref, M_INIT)
            hr_ref[...] = jnp.zeros_like(hr_ref)

        xn_ref[...] = _rms(x2_ref[prev], g1_ref[...]).astype(BF16)
        gcol = jnp.broadcast_to(bgc_ref[...], (t, GATE_LANES))
        for h in range(HEADS):
            if h:
                _ffn_gate_up((h - 1,), xn_ref, gu_ref, wg_ref, wu_ref)
            sl = _col_block(h)
            xm = xm_ref[:, sl]
            xc = jax.nn.silu(_causal_conv(xm_tail, xm, mcw_ref[:, sl], mcb_ref[:, sl], sl))
            qk = _dot(xc.astype(BF16), wqk_ref[h])
            q, k = qk[:, :HEAD_DIM], qk[:, HEAD_DIM:]
            v = _dot(xm.astype(BF16), wv_ref[h])
            gcol = gcol + _dot(xc.astype(BF16), gfold_ref[0, sl, :]) + _dot(xm.astype(BF16), gfold_ref[1, sl, :])
            xc_ref[:, sl] = xc
            q_ref[:, sl] = q
            k_ref[:, sl] = k
            v_ref[:, sl] = v
        for g in range(HEADS):
            _ffn_gate_up((HEADS - 1 + g,), xn_ref, gu_ref, wg_ref, wu_ref)
            sl = _col_block(g)
            xrc = _causal_conv(xr_tail, xr_ref[:, sl], rcw_ref[:, sl], rcb_ref[:, sl], sl)
            xrc_ref[:, sl] = xrc
            pre_ref[:, 2 * g * HEAD_DIM:(2 * g + 2) * HEAD_DIM] = _dot(xrc.astype(BF16), wax_ref[g])

        tri = tri_ref[...]
        causal = tri > 0.0
        bcol = _cumulative_log_forget(tri, gcol)
        lane = lax.broadcasted_iota(jnp.int32, (1, GATE_LANES), 1)
        grow = jnp.where(lane < HEADS, gcol, bcol).T
        _ffn_gate_up(range(2 * HEADS - 1, N_FF_CHUNKS), xn_ref, gu_ref, wg_ref, wu_ref)
        for h in range(HEADS):
            sl = _col_block(h)
            outm_ref[:, sl] = _mlstm_head(h, q_ref[:, sl], k_ref[:, sl], v_ref[:, sl], xc_ref[:, sl], zm_ref[:, sl],
                                          gcol, bcol, grow, causal, ct_ref, n_ref, m_ref,
                                          lnw_ref[:, sl], skip_ref[:, sl])
            _ffn_act(((0, 1, 2, 3), (4, 5, 6), (7, 8), (9, 10))[h], gu_ref, h_ref)

        downs = []
        for g in range(HEADS):
            sl = _col_block(g)
            downs.append(_dot(h_ref[...], wd_ref[:, sl]))
            xrc = xrc_ref[:, sl]
            r = jax.nn.sigmoid(pre_ref[:, 2 * g * HEAD_DIM:(2 * g + 1) * HEAD_DIM] + ba_ref[:, sl])
            i = jax.nn.sigmoid(pre_ref[:, (2 * g + 1) * HEAD_DIM:(2 * g + 2) * HEAD_DIM] + bx_ref[:, sl])
            log_a = RGLRU_C * r * jax.nn.log_sigmoid(lam_ref[:, sl])
            a = jnp.exp(log_a)
            z = -jnp.tanh(log_a) * (1.0 + a * a)
            u = jnp.where(z > 0.0, z * lax.rsqrt(z), 0.0) * (i * xrc)
            hseq = _linear_scan(a, u, hr_ref, sl)
            outr_ref[:, sl] = hseq * jax.nn.gelu(yr_ref[:, sl])

        o_ref[...] = _time_unpermute(_rms(x2_ref[prev] + 0.5 * jnp.concatenate(downs, axis=1), g2_ref[...]))
        nm = _rms(outm_ref[...], onm_ref[...]).astype(BF16)
        nr = _rms(outr_ref[...], onr_ref[...]).astype(BF16)
        x2_ref[step % 2] = (x1_ref[...] + _dot(nm, wout_ref[0:D_MLSTM, :])
                            + _dot(nr, wout_ref[D_MLSTM:D_MLSTM + D_RGLRU, :]))


def _row_chunks(shape):
    rows, cols = shape
    return pl.BlockSpec((rows // SETUP_STEPS, cols), lambda i: (jnp.minimum(i, SETUP_STEPS - 1), 0))


def _resident(shape):
    nd = len(shape)
    return pl.BlockSpec(shape, lambda *_: (0,) * nd, pipeline_mode=pl.Buffered(1))


def _blockdiag_dense(w):
    rows = w.reshape(HEADS, HEAD_DIM, QKV_BLOCK).astype(F32)
    col = jnp.arange(HEAD_DIM)
    spread = (col[None, :] % QKV_BLOCK == jnp.arange(QKV_BLOCK)[:, None]).astype(F32)
    dense = jnp.einsum('hro,oc->hrc', rows, spread, precision=lax.Precision.HIGHEST)
    same_block = col[:, None] // QKV_BLOCK == col[None, :] // QKV_BLOCK
    return jnp.where(same_block[None], dense, 0.0)


def _row(v):
    return v.reshape(1, -1).astype(F32)


def _permuted_causal():
    rho = jnp.arange(TILE)
    time = (rho % SUBLANES) * SEG + rho // SUBLANES
    return (time[None, :] <= time[:, None]).astype(F32)


def kernel(x, norm_ffn1, ffn1_wg, ffn1_wu, ffn1_wd, norm_mix, w_in, m_conv_w, m_conv_b, m_wq, m_wk, m_wv, m_w_gates, m_b_gates, m_ln_w, m_skip, r_conv_w, r_conv_b, r_w_a, r_b_a, r_w_x, r_b_x, r_lam, out_norm_m, out_norm_r, w_out, norm_ffn2, ffn2_wg, ffn2_wu, ffn2_wd, norm_final):
    bsz, seq, _ = x.shape
    n_tok = bsz * seq
    assert norm_ffn1.shape[0] == 1, "single-layer block"
    assert seq % TILE == 0
    d_proj = w_in.shape[-1]
    n_tiles = n_tok // TILE
    xf = x.reshape(n_tok, D_MODEL)
    params = pltpu.CompilerParams(dimension_semantics=("arbitrary",), vmem_limit_bytes=VMEM_LIMIT)

    x1, proj = pl.pallas_call(
        _ffn_in_kernel,
        grid=(SETUP_STEPS + n_tiles,),
        in_specs=[
            pl.BlockSpec((TILE, D_MODEL), lambda i: (jnp.maximum(i - SETUP_STEPS, 0), 0)),
            _resident((1, D_MODEL)),
            _row_chunks((D_MODEL, D_FF)),
            _row_chunks((D_MODEL, D_FF)),
            _row_chunks((D_FF, D_MODEL)),
            _resident((1, D_MODEL)),
            _row_chunks((D_MODEL, d_proj)),
        ],
        out_specs=[pl.BlockSpec((TILE, D_MODEL), lambda i: (jnp.maximum(i - SETUP_STEPS, 0), 0)),
                   pl.BlockSpec((TILE, d_proj), lambda i: (jnp.maximum(i - SETUP_STEPS, 0), 0))],
        out_shape=[jax.ShapeDtypeStruct((n_tok, D_MODEL), F32),
                   jax.ShapeDtypeStruct((n_tok, d_proj), F32)],
        scratch_shapes=[pltpu.VMEM((D_MODEL, D_FF), BF16), pltpu.VMEM((D_MODEL, D_FF), BF16),
                        pltpu.VMEM((D_FF, D_MODEL), BF16), pltpu.VMEM((D_MODEL, d_proj), BF16)],
        compiler_params=params,
        name="ffn1_inproj",
    )(xf, _row(norm_ffn1[0]), ffn1_wg[0], ffn1_wu[0], ffn1_wd[0], _row(norm_mix[0]), w_in[0])

    wqk = jnp.concatenate([_blockdiag_dense(m_wq[0]), _blockdiag_dense(m_wk[0])], axis=-1).astype(BF16)
    wv = _blockdiag_dense(m_wv[0]).astype(BF16)
    wg3 = m_w_gates[0].reshape(3, D_MLSTM, N_GATES)
    wgc = jnp.pad(wg3, ((0, 0), (0, 0), (0, GATE_LANES - N_GATES))).astype(BF16)
    bgc = jnp.pad(m_b_gates[0].astype(F32), (0, GATE_LANES - N_GATES)).reshape(1, GATE_LANES)
    wax = jnp.concatenate([r_w_a[0], r_w_x[0]], axis=-1).astype(BF16)

    last = n_tiles - 1
    tile_spec = lambda j: pl.BlockSpec((TILE, D_MODEL), lambda i, j=j: (jnp.clip(i - SETUP_STEPS, 0, last), j))
    weights = [
        _permuted_causal(),
        m_conv_w[0].astype(F32), _row(m_conv_b[0]), wqk, wv, wgc, bgc,
        _row(m_ln_w[0]), _row(m_skip[0]),
        r_conv_w[0].astype(F32), _row(r_conv_b[0]), wax, _row(r_b_a[0]), _row(r_b_x[0]), _row(r_lam[0]),
        _row(out_norm_m[0]), _row(out_norm_r[0]),
        _row(norm_ffn2[0]),
    ]
    streamed = [ffn2_wg[0], ffn2_wu[0], ffn2_wd[0], w_out[0]]
    tile_f32 = pltpu.VMEM((TILE, D_MODEL), F32)
    out = pl.pallas_call(
        functools.partial(_mix_kernel, tiles_per_seq=seq // TILE),
        grid=(SETUP_STEPS + n_tiles + 1,),
        in_specs=[tile_spec(0), tile_spec(1), tile_spec(2), tile_spec(3), tile_spec(0)]
                 + [_resident(w.shape) for w in weights] + [_row_chunks(w.shape) for w in streamed]
                 + [_resident((1, D_MODEL))],
        out_specs=pl.BlockSpec((TILE, D_MODEL), lambda i: (jnp.maximum(i - SETUP_STEPS - 1, 0), 0)),
        out_shape=jax.ShapeDtypeStruct((n_tok, D_MODEL), F32),
        scratch_shapes=[
            pltpu.VMEM((WRAP_ROWS, D_MLSTM), F32),
            pltpu.VMEM((WRAP_ROWS, D_RGLRU), F32),
            pltpu.VMEM((HEADS, HEAD_DIM, HEAD_DIM), F32),
            pltpu.VMEM((HEADS, HEAD_DIM), F32),
            pltpu.VMEM((HEADS, GATE_LANES), F32),
            pltpu.VMEM((SUBLANES, D_RGLRU), F32),
            pltpu.VMEM((2, TILE, D_MODEL), F32),
            pltpu.VMEM((TILE, D_MODEL), BF16),
            pltpu.VMEM((2, TILE, D_FF), F32),
            pltpu.VMEM((TILE, D_FF), BF16),
            tile_f32, tile_f32, tile_f32,
            tile_f32,
            tile_f32,
            pltpu.VMEM((TILE, 2 * D_RGLRU), F32),
            tile_f32, tile_f32,
            pltpu.VMEM((D_MODEL, D_FF), BF16), pltpu.VMEM((D_MODEL, D_FF), BF16),
            pltpu.VMEM((D_FF, D_MODEL), BF16),
            pltpu.VMEM((D_MLSTM + D_RGLRU, D_MODEL), BF16),
            pltpu.VMEM((2, D_MLSTM, GATE_LANES), BF16),
        ],
        compiler_params=params,
        name="mixers_ffn2",
    )(proj, proj, proj, proj, x1, *weights, *streamed, _row(norm_final))
    return out.reshape(bsz, seq, D_MODEL)
```

```python
import functools

import jax
import jax.numpy as jnp
from jax import lax
from jax.experimental import pallas as pl
from jax.experimental.pallas import tpu as pltpu

D_MODEL = 1024
D_FF = 2816
D_MLSTM = 1024
D_RGLRU = 1024
HEADS = 4
HEAD_DIM = 256
QKV_BLOCK = 4
CONV_W = 4
RGLRU_C = 8.0
EPS = 1e-6
M_INIT = -1e30
N_GATES = 2 * HEADS

SUBLANES = 8
GATE_LANES = 128
TILE = 256
SEG = TILE // SUBLANES
WRAP_ROWS = (CONV_W - 1) * SUBLANES
FF_CHUNK = 256
N_FF_CHUNKS = D_FF // FF_CHUNK
FFN1_TILES_PER_STEP = 2
SETUP_STEPS = 16
VMEM_LIMIT = 60 * 1024 * 1024

F32 = jnp.float32
BF16 = jnp.bfloat16


def _time_permute(x):
    t, d = x.shape
    return x.reshape(SUBLANES, t // SUBLANES, d).swapaxes(0, 1).reshape(t, d)


def _time_unpermute(x):
    t, d = x.shape
    return x.reshape(t // SUBLANES, SUBLANES, d).swapaxes(0, 1).reshape(t, d)


def _rms(x, g):
    return x * lax.rsqrt(jnp.mean(x * x, axis=-1, keepdims=True) + EPS) * g


def _dot(a, b):
    return jnp.dot(a, b, preferred_element_type=F32)


def _cast_weight_chunk(step, src_ref, dst_ref):
    n = src_ref.shape[0]
    dst_ref[pl.ds(pl.multiple_of(step * n, n), n), :] = src_ref[...].astype(BF16)


def _col_block(i):
    return slice(i * HEAD_DIM, (i + 1) * HEAD_DIM)


def _ffn_in_kernel(x_ref, g1_ref, wg32_ref, wu32_ref, wd32_ref, g2_ref, win32_ref, x1_ref, proj_ref,
                   wg_ref, wu_ref, wd_ref, win_ref):
    step = pl.program_id(0)

    @pl.when(step < SETUP_STEPS)
    def _load_weights():
        for src, dst in ((wg32_ref, wg_ref), (wu32_ref, wu_ref), (wd32_ref, wd_ref), (win32_ref, win_ref)):
            _cast_weight_chunk(step, src, dst)

    @pl.when(step >= SETUP_STEPS)
    def _tiles():
        for sub in range(FFN1_TILES_PER_STEP):
            rows = slice(sub * TILE, (sub + 1) * TILE)
            x = _time_permute(x_ref[rows, :])
            xn = _rms(x, g1_ref[...]).astype(BF16)
            h = (jax.nn.silu(_dot(xn, wg_ref[...])) * _dot(xn, wu_ref[...])).astype(BF16)
            y = x + 0.5 * _dot(h, wd_ref[...])
            x1_ref[rows, :] = y
            proj_ref[rows, :] = _dot(_rms(y, g2_ref[...]).astype(BF16), win_ref[...])


def _ffn_gate_up(chunks, xn_ref, gu_ref, wg_ref, wu_ref):
    for c in chunks:
        sl = slice(c * FF_CHUNK, (c + 1) * FF_CHUNK)
        xn = xn_ref[...]
        gu_ref[0, :, sl] = _dot(xn, wg_ref[:, sl])
        gu_ref[1, :, sl] = _dot(xn, wu_ref[:, sl])


def _ffn_act(chunks, gu_ref, h_ref):
    for c in chunks:
        sl = slice(c * FF_CHUNK, (c + 1) * FF_CHUNK)
        h_ref[:, sl] = (jax.nn.silu(gu_ref[0, :, sl]) * gu_ref[1, :, sl]).astype(BF16)


def _causal_conv(tail_ref, x, w, bias, sl):
    t = x.shape[0]
    sub = lax.broadcasted_iota(jnp.int32, (SUBLANES, 1), 0)
    wrap = []
    for j in range(CONV_W - 1):
        cur = x[t - WRAP_ROWS + j * SUBLANES:t - WRAP_ROWS + (j + 1) * SUBLANES, :]
        prev = tail_ref[j * SUBLANES:(j + 1) * SUBLANES, sl]
        wrap.append(pltpu.roll(jnp.where(sub == SUBLANES - 1, prev, cur), 1, 0))
    tail_ref[:, sl] = x[t - WRAP_ROWS:t, :]
    ext = jnp.concatenate(wrap + [x], axis=0)
    y = bias
    for tap in range(CONV_W - 1):
        lo = WRAP_ROWS - (CONV_W - 1 - tap) * SUBLANES
        y = y + w[tap:tap + 1, :] * ext[lo:lo + t, :]
    return y + w[CONV_W - 1:CONV_W, :] * x


def _linear_scan(a, u, state_ref, sl):
    t = a.shape[0]
    hs, ps = [], []
    h = p = None
    for r in range(t // SUBLANES):
        a_r = a[r * SUBLANES:(r + 1) * SUBLANES, :]
        u_r = u[r * SUBLANES:(r + 1) * SUBLANES, :]
        h = u_r if h is None else a_r * h + u_r
        p = a_r if p is None else a_r * p
        hs.append(h)
        ps.append(p)
    sub = lax.broadcasted_iota(jnp.int32, (SUBLANES, 1), 0)
    a_seg, u_seg = p, h
    d = 1
    while d < SUBLANES:
        a_sh = jnp.where(sub < d, 1.0, pltpu.roll(a_seg, d, 0))
        u_sh = jnp.where(sub < d, 0.0, pltpu.roll(u_seg, d, 0))
        u_seg = a_seg * u_sh + u_seg
        a_seg = a_seg * a_sh
        d *= 2
    carry_in = state_ref[0:1, sl]
    seg_end = u_seg + a_seg * carry_in
    state_ref[0:1, sl] = seg_end[SUBLANES - 1:SUBLANES, :]
    seg_start = jnp.where(sub == 0, carry_in, pltpu.roll(seg_end, 1, 0))
    return jnp.concatenate([h_r + p_r * seg_start for h_r, p_r in zip(hs, ps)], axis=0)


def _cumulative_log_forget(tri, gcol):
    lf = jax.nn.log_sigmoid(gcol)
    hi = lf.astype(BF16)
    rest = lf - hi.astype(F32)
    mid = rest.astype(BF16)
    lo = (rest - mid.astype(F32)).astype(BF16)
    tri_b = tri.astype(BF16)
    return _dot(tri_b, hi) + _dot(tri_b, mid) + _dot(tri_b, lo)


def _mlstm_head(h, q, k, v, xc_h, z_h, gcol, bcol, grow, causal,
                ct_ref, n_ref, m_ref, lnw, skip):
    t = q.shape[0]
    qb = q.astype(BF16)
    ks = k * (HEAD_DIM ** -0.5)
    kb = ks.astype(BF16)
    b_i = bcol[:, HEADS + h:HEADS + h + 1]
    ig_i = gcol[:, h:h + 1]
    b_j = grow[HEADS + h:HEADS + h + 1, :]
    ig_j = grow[h:h + 1, :]
    m_prev = m_ref[h:h + 1, 0:1]

    dlog = jnp.where(causal, b_i - b_j + ig_j, -jnp.inf)
    inter = b_i + m_prev
    m_row = jnp.maximum(inter, jnp.max(dlog, axis=-1, keepdims=True))
    s = lax.dot_general(qb, kb, (((1,), (1,)), ((), ())), preferred_element_type=F32)
    s = s * jnp.exp(dlog - m_row)
    inter_w = jnp.exp(inter - m_row)
    num = _dot(s.astype(BF16), v.astype(BF16)) + inter_w * _dot(qb, ct_ref[h].astype(BF16))
    den = (jnp.sum(s, axis=-1, keepdims=True)
           + inter_w * jnp.sum(q * n_ref[h:h + 1, :], axis=-1, keepdims=True))
    hh = num / jnp.maximum(jnp.abs(den), jnp.exp(-m_row))

    g = b_i[t - 1:t, :]
    w = g - b_i + ig_i
    m_new = jnp.maximum(g + m_prev, jnp.max(w, axis=0, keepdims=True))
    decay = jnp.exp(g + m_prev - m_new)
    wk = jnp.exp(w - m_new)
    ct_ref[h] = decay * ct_ref[h] + lax.dot_general(
        kb, (v * wk).astype(BF16), (((0,), (0,)), ((), ())), preferred_element_type=F32)
    n_ref[h:h + 1, :] = decay * n_ref[h:h + 1, :] + jnp.sum(ks * wk, axis=0, keepdims=True)
    m_ref[h:h + 1, :] = jnp.broadcast_to(m_new, (1, m_ref.shape[1]))

    mu = jnp.mean(hh, axis=-1, keepdims=True)
    var = jnp.mean(jnp.square(hh - mu), axis=-1, keepdims=True)
    hn = (hh - mu) * lax.rsqrt(var + EPS) * lnw
    return (hn + skip * xc_h) * jax.nn.silu(z_h)


def _mix_kernel(xm_ref, zm_ref, xr_ref, yr_ref, x1_ref, tri_ref,
                mcw_ref, mcb_ref, wqk_ref, wv_ref, wgc_ref, bgc_ref,
                lnw_ref, skip_ref,
                rcw_ref, rcb_ref, wax_ref, ba_ref, bx_ref, lam_ref,
                onm_ref, onr_ref,
                g1_ref, wg32_ref, wu32_ref, wd32_ref, wout32_ref, g2_ref,
                o_ref,
                xm_tail, xr_tail, ct_ref, n_ref, m_ref, hr_ref, x2_ref,
                xn_ref, gu_ref, h_ref, q_ref, k_ref, v_ref, xc_ref, xrc_ref, pre_ref, outm_ref, outr_ref,
                wg_ref, wu_ref, wd_ref, wout_ref, gfold_ref,
                *, tiles_per_seq):
    grid_step = pl.program_id(0)

    @pl.when(grid_step < SETUP_STEPS)
    def _load_weights():
        for src, dst in ((wg32_ref, wg_ref), (wu32_ref, wu_ref), (wd32_ref, wd_ref), (wout32_ref, wout_ref)):
            _cast_weight_chunk(grid_step, src, dst)

        @pl.when(grid_step == 0)
        def _fold_gate_weights():
            for h in range(HEADS):
                sl = _col_block(h)
                wqk = wqk_ref[h]
                gfold_ref[0, sl, :] = (_dot(wqk[:, :HEAD_DIM], wgc_ref[0, sl, :])
                                       + _dot(wqk[:, HEAD_DIM:], wgc_ref[1, sl, :])).astype(BF16)
                gfold_ref[1, sl, :] = _dot(wv_ref[h], wgc_ref[2, sl, :]).astype(BF16)

    @pl.when(grid_step >= SETUP_STEPS)
    def _tile():
        t = xm_ref.shape[0]
        step = pl.program_id(0) - SETUP_STEPS
        prev = (step + 1) % 2

        @pl.when(step == 0)
        def _init_handoff():
            x2_ref[...] = jnp.zeros_like(x2_ref)

        @pl.when(step % tiles_per_seq == 0)
        def _reset_state():
            xm_tail[...] = jnp.zeros_like(xm_tail)
            xr_tail[...] = jnp.zeros_like(xr_tail)
            ct_ref[...] = jnp.zeros_like(ct_ref)
            n_ref[...] = jnp.zeros_like(n_ref)
            m_ref[...] = jnp.full_like(m_ref, M_INIT)
            hr_ref[...] = jnp.zeros_like(hr_ref)

        xn_ref[...] = _rms(x2_ref[prev], g1_ref[...]).astype(BF16)
        gcol = jnp.broadcast_to(bgc_ref[...], (t, GATE_LANES))
        for h in range(HEADS):
            if h:
                _ffn_gate_up((h - 1,), xn_ref, gu_ref, wg_ref, wu_ref)
            sl = _col_block(h)
            xm = xm_ref[:, sl]
            xc = jax.nn.silu(_causal_conv(xm_tail, xm, mcw_ref[:, sl], mcb_ref[:, sl], sl))
            qk = _dot(xc.astype(BF16), wqk_ref[h])
            q, k = qk[:, :HEAD_DIM], qk[:, HEAD_DIM:]
            v = _dot(xm.astype(BF16), wv_ref[h])
            gcol = gcol + _dot(xc.astype(BF16), gfold_ref[0, sl, :]) + _dot(xm.astype(BF16), gfold_ref[1, sl, :])
            xc_ref[:, sl] = xc
            q_ref[:, sl] = q
            k_ref[:, sl] = k
            v_ref[:, sl] = v
        for g in range(HEADS):
            _ffn_gate_up((HEADS - 1 + g,), xn_ref, gu_ref, wg_ref, wu_ref)
            sl = _col_block(g)
            xrc = _causal_conv(xr_tail, xr_ref[:, sl], rcw_ref[:, sl], rcb_ref[:, sl], sl)
            xrc_ref[:, sl] = xrc
            pre_ref[:, 2 * g * HEAD_DIM:(2 * g + 2) * HEAD_DIM] = _dot(xrc.astype(BF16), wax_ref[g])

        tri = tri_ref[...]
        causal = tri > 0.0
        bcol = _cumulative_log_forget(tri, gcol)
        lane = lax.broadcasted_iota(jnp.int32, (1, GATE_LANES), 1)
        grow = jnp.where(lane < HEADS, gcol, bcol).T
        _ffn_gate_up(range(2 * HEADS - 1, N_FF_CHUNKS), xn_ref, gu_ref, wg_ref, wu_ref)
        for h in range(HEADS):
            sl = _col_block(h)
            outm_ref[:, sl] = _mlstm_head(h, q_ref[:, sl], k_ref[:, sl], v_ref[:, sl], xc_ref[:, sl], zm_ref[:, sl],
                                          gcol, bcol, grow, causal, ct_ref, n_ref, m_ref,
                                          lnw_ref[:, sl], skip_ref[:, sl])
            _ffn_act(((0, 1, 2, 3), (4, 5, 6), (7, 8), (9, 10))[h], gu_ref, h_ref)

        downs = []
        for g in range(HEADS):
            sl = _col_block(g)
            downs.append(_dot(h_ref[...], wd_ref[:, sl]))
            xrc = xrc_ref[:, sl]
            r = jax.nn.sigmoid(pre_ref[:, 2 * g * HEAD_DIM:(2 * g + 1) * HEAD_DIM] + ba_ref[:, sl])
            i = jax.nn.sigmoid(pre_ref[:, (2 * g + 1) * HEAD_DIM:(2 * g + 2) * HEAD_DIM] + bx_ref[:, sl])
            log_a = RGLRU_C * r * jax.nn.log_sigmoid(lam_ref[:, sl])
            a = jnp.exp(log_a)
            z = -jnp.tanh(log_a) * (1.0 + a * a)
            u = jnp.where(z > 0.0, z * lax.rsqrt(z), 0.0) * (i * xrc)
            hseq = _linear_scan(a, u, hr_ref, sl)
            outr_ref[:, sl] = hseq * jax.nn.gelu(yr_ref[:, sl])

        o_ref[...] = _time_unpermute(_rms(x2_ref[prev] + 0.5 * jnp.concatenate(downs, axis=1), g2_ref[...]))
        nm = _rms(outm_ref[...], onm_ref[...]).astype(BF16)
        nr = _rms(outr_ref[...], onr_ref[...]).astype(BF16)
        x2_ref[step % 2] = (x1_ref[...] + _dot(nm, wout_ref[0:D_MLSTM, :])
                            + _dot(nr, wout_ref[D_MLSTM:D_MLSTM + D_RGLRU, :]))


def _row_chunks(shape):
    rows, cols = shape
    return pl.BlockSpec((rows // SETUP_STEPS, cols), lambda i: (jnp.minimum(i, SETUP_STEPS - 1), 0))


def _resident(shape):
    nd = len(shape)
    return pl.BlockSpec(shape, lambda *_: (0,) * nd, pipeline_mode=pl.Buffered(1))


def _blockdiag_dense(w):
    rows = w.reshape(HEADS, HEAD_DIM, QKV_BLOCK).astype(F32)
    col = jnp.arange(HEAD_DIM)
    spread = (col[None, :] % QKV_BLOCK == jnp.arange(QKV_BLOCK)[:, None]).astype(F32)
    dense = jnp.einsum('hro,oc->hrc', rows, spread, precision=lax.Precision.HIGHEST)
    same_block = col[:, None] // QKV_BLOCK == col[None, :] // QKV_BLOCK
    return jnp.where(same_block[None], dense, 0.0)


def _row(v):
    return v.reshape(1, -1).astype(F32)


def _permuted_causal():
    rho = jnp.arange(TILE)
    time = (rho % SUBLANES) * SEG + rho // SUBLANES
    return (time[None, :] <= time[:, None]).astype(F32)


def kernel(x, norm_ffn1, ffn1_wg, ffn1_wu, ffn1_wd, norm_mix, w_in, m_conv_w, m_conv_b, m_wq, m_wk, m_wv, m_w_gates, m_b_gates, m_ln_w, m_skip, r_conv_w, r_conv_b, r_w_a, r_b_a, r_w_x, r_b_x, r_lam, out_norm_m, out_norm_r, w_out, norm_ffn2, ffn2_wg, ffn2_wu, ffn2_wd, norm_final):
    bsz, seq, _ = x.shape
    n_tok = bsz * seq
    assert norm_ffn1.shape[0] == 1, "single-layer block"
    assert seq % TILE == 0
    d_proj = w_in.shape[-1]
    n_tiles = n_tok // TILE
    xf = x.reshape(n_tok, D_MODEL)
    params = pltpu.CompilerParams(dimension_semantics=("arbitrary",), vmem_limit_bytes=VMEM_LIMIT)

    assert n_tiles % FFN1_TILES_PER_STEP == 0
    step_rows = FFN1_TILES_PER_STEP * TILE
    x1, proj = pl.pallas_call(
        _ffn_in_kernel,
        grid=(SETUP_STEPS + n_tiles // FFN1_TILES_PER_STEP,),
        in_specs=[
            pl.BlockSpec((step_rows, D_MODEL), lambda i: (jnp.maximum(i - SETUP_STEPS, 0), 0)),
            _resident((1, D_MODEL)),
            _row_chunks((D_MODEL, D_FF)),
            _row_chunks((D_MODEL, D_FF)),
            _row_chunks((D_FF, D_MODEL)),
            _resident((1, D_MODEL)),
            _row_chunks((D_MODEL, d_proj)),
        ],
        out_specs=[pl.BlockSpec((step_rows, D_MODEL), lambda i: (jnp.maximum(i - SETUP_STEPS, 0), 0)),
                   pl.BlockSpec((step_rows, d_proj), lambda i: (jnp.maximum(i - SETUP_STEPS, 0), 0))],
        out_shape=[jax.ShapeDtypeStruct((n_tok, D_MODEL), F32),
                   jax.ShapeDtypeStruct((n_tok, d_proj), F32)],
        scratch_shapes=[pltpu.VMEM((D_MODEL, D_FF), BF16), pltpu.VMEM((D_MODEL, D_FF), BF16),
                        pltpu.VMEM((D_FF, D_MODEL), BF16), pltpu.VMEM((D_MODEL, d_proj), BF16)],
        compiler_params=params,
        name="ffn1_inproj",
    )(xf, _row(norm_ffn1[0]), ffn1_wg[0], ffn1_wu[0], ffn1_wd[0], _row(norm_mix[0]), w_in[0])

    wqk = jnp.concatenate([_blockdiag_dense(m_wq[0]), _blockdiag_dense(m_wk[0])], axis=-1).astype(BF16)
    wv = _blockdiag_dense(m_wv[0]).astype(BF16)
    wg3 = m_w_gates[0].reshape(3, D_MLSTM, N_GATES)
    wgc = jnp.pad(wg3, ((0, 0), (0, 0), (0, GATE_LANES - N_GATES))).astype(BF16)
    bgc = jnp.pad(m_b_gates[0].astype(F32), (0, GATE_LANES - N_GATES)).reshape(1, GATE_LANES)
    wax = jnp.concatenate([r_w_a[0], r_w_x[0]], axis=-1).astype(BF16)

    last = n_tiles - 1
    tile_spec = lambda j: pl.BlockSpec((TILE, D_MODEL), lambda i, j=j: (jnp.clip(i - SETUP_STEPS, 0, last), j))
    weights = [
        _permuted_causal(),
        m_conv_w[0].astype(F32), _row(m_conv_b[0]), wqk, wv, wgc, bgc,
        _row(m_ln_w[0]), _row(m_skip[0]),
        r_conv_w[0].astype(F32), _row(r_conv_b[0]), wax, _row(r_b_a[0]), _row(r_b_x[0]), _row(r_lam[0]),
        _row(out_norm_m[0]), _row(out_norm_r[0]),
        _row(norm_ffn2[0]),
    ]
    streamed = [ffn2_wg[0], ffn2_wu[0], ffn2_wd[0], w_out[0]]
    tile_f32 = pltpu.VMEM((TILE, D_MODEL), F32)
    out = pl.pallas_call(
        functools.partial(_mix_kernel, tiles_per_seq=seq // TILE),
        grid=(SETUP_STEPS + n_tiles + 1,),
        in_specs=[tile_spec(0), tile_spec(1), tile_spec(2), tile_spec(3), tile_spec(0)]
                 + [_resident(w.shape) for w in weights] + [_row_chunks(w.shape) for w in streamed]
                 + [_resident((1, D_MODEL))],
        out_specs=pl.BlockSpec((TILE, D_MODEL), lambda i: (jnp.maximum(i - SETUP_STEPS - 1, 0), 0)),
        out_shape=jax.ShapeDtypeStruct((n_tok, D_MODEL), F32),
        scratch_shapes=[
            pltpu.VMEM((WRAP_ROWS, D_MLSTM), F32),
            pltpu.VMEM((WRAP_ROWS, D_RGLRU), F32),
            pltpu.VMEM((HEADS, HEAD_DIM, HEAD_DIM), F32),
            pltpu.VMEM((HEADS, HEAD_DIM), F32),
            pltpu.VMEM((HEADS, GATE_LANES), F32),
            pltpu.VMEM((SUBLANES, D_RGLRU), F32),
            pltpu.VMEM((2, TILE, D_MODEL), F32),
            pltpu.VMEM((TILE, D_MODEL), BF16),
            pltpu.VMEM((2, TILE, D_FF), F32),
            pltpu.VMEM((TILE, D_FF), BF16),
            tile_f32, tile_f32, tile_f32,
            tile_f32,
            tile_f32,
            pltpu.VMEM((TILE, 2 * D_RGLRU), F32),
            tile_f32, tile_f32,
            pltpu.VMEM((D_MODEL, D_FF), BF16), pltpu.VMEM((D_MODEL, D_FF), BF16),
            pltpu.VMEM((D_FF, D_MODEL), BF16),
            pltpu.VMEM((D_MLSTM + D_RGLRU, D_MODEL), BF16),
            pltpu.VMEM((2, D_MLSTM, GATE_LANES), BF16),
        ],
        compiler_params=params,
        name="mixers_ffn2",
    )(proj, proj, proj, proj, x1, *weights, *streamed, _row(norm_final))
    return out.reshape(bsz, seq, D_MODEL)
```

```python
import functools

import jax
import jax.numpy as jnp
from jax import lax
from jax.experimental import pallas as pl
from jax.experimental.pallas import tpu as pltpu

D_MODEL = 1024
D_FF = 2816
D_MLSTM = 1024
D_RGLRU = 1024
HEADS = 4
HEAD_DIM = 256
QKV_BLOCK = 4
CONV_W = 4
RGLRU_C = 8.0
EPS = 1e-6
M_INIT = -1e30
N_GATES = 2 * HEADS

SUBLANES = 8
GATE_LANES = 128
TILE = 256
SEG = TILE // SUBLANES
WRAP_ROWS = (CONV_W - 1) * SUBLANES
FF_CHUNK = 256
N_FF_CHUNKS = D_FF // FF_CHUNK
FFN1_TILES_PER_STEP = 2
SETUP_STEPS = 16
VMEM_LIMIT = 60 * 1024 * 1024

F32 = jnp.float32
BF16 = jnp.bfloat16


def _time_permute(x):
    t, d = x.shape
    return x.reshape(SUBLANES, t // SUBLANES, d).swapaxes(0, 1).reshape(t, d)


def _time_unpermute(x):
    t, d = x.shape
    return x.reshape(t // SUBLANES, SUBLANES, d).swapaxes(0, 1).reshape(t, d)


def _rms(x, g):
    return x * lax.rsqrt(jnp.mean(x * x, axis=-1, keepdims=True) + EPS) * g


def _dot(a, b):
    return jnp.dot(a, b, preferred_element_type=F32)


def _cast_weight_chunk(step, src_ref, dst_ref):
    n = src_ref.shape[0]
    dst_ref[pl.ds(pl.multiple_of(step * n, n), n), :] = src_ref[...].astype(BF16)


def _col_block(i):
    return slice(i * HEAD_DIM, (i + 1) * HEAD_DIM)


def _ffn_in_kernel(x_ref, g1_ref, wg32_ref, wu32_ref, wd32_ref, g2_ref, win32_ref, x1_ref, proj_ref,
                   wg_ref, wu_ref, wd_ref, win_ref):
    step = pl.program_id(0)

    @pl.when(step < SETUP_STEPS)
    def _load_weights():
        for src, dst in ((wg32_ref, wg_ref), (wu32_ref, wu_ref), (wd32_ref, wd_ref), (win32_ref, win_ref)):
            _cast_weight_chunk(step, src, dst)

    @pl.when(step >= SETUP_STEPS)
    def _tiles():
        for sub in range(FFN1_TILES_PER_STEP):
            rows = slice(sub * TILE, (sub + 1) * TILE)
            x = _time_permute(x_ref[rows, :])
            xn = _rms(x, g1_ref[...]).astype(BF16)
            h = (jax.nn.silu(_dot(xn, wg_ref[...])) * _dot(xn, wu_ref[...])).astype(BF16)
            y = x + 0.5 * _dot(h, wd_ref[...])
            x1_ref[rows, :] = y
            p = _dot(_rms(y, g2_ref[...]).astype(BF16), win_ref[...])
            for j in range(proj_ref.shape[0]):
                proj_ref[j, rows, :] = p[:, j * D_MODEL:(j + 1) * D_MODEL]


def _ffn_gate_up(chunks, xn_ref, gu_ref, wg_ref, wu_ref):
    for c in chunks:
        sl = slice(c * FF_CHUNK, (c + 1) * FF_CHUNK)
        xn = xn_ref[...]
        gu_ref[0, :, sl] = _dot(xn, wg_ref[:, sl])
        gu_ref[1, :, sl] = _dot(xn, wu_ref[:, sl])


def _ffn_act(chunks, gu_ref, h_ref):
    for c in chunks:
        sl = slice(c * FF_CHUNK, (c + 1) * FF_CHUNK)
        h_ref[:, sl] = (jax.nn.silu(gu_ref[0, :, sl]) * gu_ref[1, :, sl]).astype(BF16)


def _causal_conv(tail_ref, x, w, bias, sl):
    t = x.shape[0]
    sub = lax.broadcasted_iota(jnp.int32, (SUBLANES, 1), 0)
    wrap = []
    for j in range(CONV_W - 1):
        cur = x[t - WRAP_ROWS + j * SUBLANES:t - WRAP_ROWS + (j + 1) * SUBLANES, :]
        prev = tail_ref[j * SUBLANES:(j + 1) * SUBLANES, sl]
        wrap.append(pltpu.roll(jnp.where(sub == SUBLANES - 1, prev, cur), 1, 0))
    tail_ref[:, sl] = x[t - WRAP_ROWS:t, :]
    ext = jnp.concatenate(wrap + [x], axis=0)
    y = bias
    for tap in range(CONV_W - 1):
        lo = WRAP_ROWS - (CONV_W - 1 - tap) * SUBLANES
        y = y + w[tap:tap + 1, :] * ext[lo:lo + t, :]
    return y + w[CONV_W - 1:CONV_W, :] * x


def _linear_scan(a, u, state_ref, sl):
    t = a.shape[0]
    hs, ps = [], []
    h = p = None
    for r in range(t // SUBLANES):
        a_r = a[r * SUBLANES:(r + 1) * SUBLANES, :]
        u_r = u[r * SUBLANES:(r + 1) * SUBLANES, :]
        h = u_r if h is None else a_r * h + u_r
        p = a_r if p is None else a_r * p
        hs.append(h)
        ps.append(p)
    sub = lax.broadcasted_iota(jnp.int32, (SUBLANES, 1), 0)
    a_seg, u_seg = p, h
    d = 1
    while d < SUBLANES:
        a_sh = jnp.where(sub < d, 1.0, pltpu.roll(a_seg, d, 0))
        u_sh = jnp.where(sub < d, 0.0, pltpu.roll(u_seg, d, 0))
        u_seg = a_seg * u_sh + u_seg
        a_seg = a_seg * a_sh
        d *= 2
    carry_in = state_ref[0:1, sl]
    seg_end = u_seg + a_seg * carry_in
    state_ref[0:1, sl] = seg_end[SUBLANES - 1:SUBLANES, :]
    seg_start = jnp.where(sub == 0, carry_in, pltpu.roll(seg_end, 1, 0))
    return jnp.concatenate([h_r + p_r * seg_start for h_r, p_r in zip(hs, ps)], axis=0)


def _cumulative_log_forget(tri, gcol):
    lf = jax.nn.log_sigmoid(gcol)
    hi = lf.astype(BF16)
    rest = lf - hi.astype(F32)
    mid = rest.astype(BF16)
    lo = (rest - mid.astype(F32)).astype(BF16)
    tri_b = tri.astype(BF16)
    return _dot(tri_b, hi) + _dot(tri_b, mid) + _dot(tri_b, lo)


def _mlstm_head(h, q, k, v, xc_h, z_h, gcol, bcol, grow, causal,
                ct_ref, n_ref, m_ref, lnw, skip):
    t = q.shape[0]
    qb = q.astype(BF16)
    ks = k * (HEAD_DIM ** -0.5)
    kb = ks.astype(BF16)
    b_i = bcol[:, HEADS + h:HEADS + h + 1]
    ig_i = gcol[:, h:h + 1]
    b_j = grow[HEADS + h:HEADS + h + 1, :]
    ig_j = grow[h:h + 1, :]
    m_prev = m_ref[h:h + 1, 0:1]

    dlog = jnp.where(causal, b_i - b_j + ig_j, -jnp.inf)
    inter = b_i + m_prev
    m_row = jnp.maximum(inter, jnp.max(dlog, axis=-1, keepdims=True))
    s = lax.dot_general(qb, kb, (((1,), (1,)), ((), ())), preferred_element_type=F32)
    s = s * jnp.exp(dlog - m_row)
    inter_w = jnp.exp(inter - m_row)
    num = _dot(s.astype(BF16), v.astype(BF16)) + inter_w * _dot(qb, ct_ref[h].astype(BF16))
    den = (jnp.sum(s, axis=-1, keepdims=True)
           + inter_w * jnp.sum(q * n_ref[h:h + 1, :], axis=-1, keepdims=True))
    hh = num / jnp.maximum(jnp.abs(den), jnp.exp(-m_row))

    g = b_i[t - 1:t, :]
    w = g - b_i + ig_i
    m_new = jnp.maximum(g + m_prev, jnp.max(w, axis=0, keepdims=True))
    decay = jnp.exp(g + m_prev - m_new)
    wk = jnp.exp(w - m_new)
    ct_ref[h] = decay * ct_ref[h] + lax.dot_general(
        kb, (v * wk).astype(BF16), (((0,), (0,)), ((), ())), preferred_element_type=F32)
    n_ref[h:h + 1, :] = decay * n_ref[h:h + 1, :] + jnp.sum(ks * wk, axis=0, keepdims=True)
    m_ref[h:h + 1, :] = jnp.broadcast_to(m_new, (1, m_ref.shape[1]))

    mu = jnp.mean(hh, axis=-1, keepdims=True)
    var = jnp.mean(jnp.square(hh - mu), axis=-1, keepdims=True)
    hn = (hh - mu) * lax.rsqrt(var + EPS) * lnw
    return (hn + skip * xc_h) * jax.nn.silu(z_h)


def _mix_kernel(xm_ref, zm_ref, xr_ref, yr_ref, x1_ref, tri_ref,
                mcw_ref, mcb_ref, wqk_ref, wv_ref, wgc_ref, bgc_ref,
                lnw_ref, skip_ref,
                rcw_ref, rcb_ref, wax_ref, ba_ref, bx_ref, lam_ref,
                onm_ref, onr_ref,
                g1_ref, wg32_ref, wu32_ref, wd32_ref, wout32_ref, g2_ref,
                o_ref,
                xm_tail, xr_tail, ct_ref, n_ref, m_ref, hr_ref, x2_ref,
                xn_ref, gu_ref, h_ref, q_ref, k_ref, v_ref, xc_ref, xrc_ref, pre_ref, outm_ref, outr_ref,
                wg_ref, wu_ref, wd_ref, wout_ref, gfold_ref,
                *, tiles_per_seq):
    grid_step = pl.program_id(0)

    @pl.when(grid_step < SETUP_STEPS)
    def _load_weights():
        for src, dst in ((wg32_ref, wg_ref), (wu32_ref, wu_ref), (wd32_ref, wd_ref), (wout32_ref, wout_ref)):
            _cast_weight_chunk(grid_step, src, dst)

        @pl.when(grid_step == 0)
        def _fold_gate_weights():
            for h in range(HEADS):
                sl = _col_block(h)
                wqk = wqk_ref[h]
                gfold_ref[0, sl, :] = (_dot(wqk[:, :HEAD_DIM], wgc_ref[0, sl, :])
                                       + _dot(wqk[:, HEAD_DIM:], wgc_ref[1, sl, :])).astype(BF16)
                gfold_ref[1, sl, :] = _dot(wv_ref[h], wgc_ref[2, sl, :]).astype(BF16)

    @pl.when(grid_step >= SETUP_STEPS)
    def _tile():
        t = xm_ref.shape[0]
        step = pl.program_id(0) - SETUP_STEPS
        prev = (step + 1) % 2

        @pl.when(step == 0)
        def _init_handoff():
            x2_ref[...] = jnp.zeros_like(x2_ref)

        @pl.when(step % tiles_per_seq == 0)
        def _reset_state():
            xm_tail[...] = jnp.zeros_like(xm_tail)
            xr_tail[...] = jnp.zeros_like(xr_tail)
            ct_ref[...] = jnp.zeros_like(ct_ref)
            n_ref[...] = jnp.zeros_like(n_ref)
            m_ref[...] = jnp.full_like(m_ref, M_INIT)
            hr_ref[...] = jnp.zeros_like(hr_ref)

        xn_ref[...] = _rms(x2_ref[prev], g1_ref[...]).astype(BF16)
        gcol = jnp.broadcast_to(bgc_ref[...], (t, GATE_LANES))
        for h in range(HEADS):
            if h:
                _ffn_gate_up((h - 1,), xn_ref, gu_ref, wg_ref, wu_ref)
            sl = _col_block(h)
            xm = xm_ref[:, sl]
            xc = jax.nn.silu(_causal_conv(xm_tail, xm, mcw_ref[:, sl], mcb_ref[:, sl], sl))
            qk = _dot(xc.astype(BF16), wqk_ref[h])
            q, k = qk[:, :HEAD_DIM], qk[:, HEAD_DIM:]
            v = _dot(xm.astype(BF16), wv_ref[h])
            gcol = gcol + _dot(xc.astype(BF16), gfold_ref[0, sl, :]) + _dot(xm.astype(BF16), gfold_ref[1, sl, :])
            xc_ref[:, sl] = xc
            q_ref[:, sl] = q
            k_ref[:, sl] = k
            v_ref[:, sl] = v
        for g in range(HEADS):
            _ffn_gate_up((HEADS - 1 + g,), xn_ref, gu_ref, wg_ref, wu_ref)
            sl = _col_block(g)
            xrc = _causal_conv(xr_tail, xr_ref[:, sl], rcw_ref[:, sl], rcb_ref[:, sl], sl)
            xrc_ref[:, sl] = xrc
            pre_ref[:, 2 * g * HEAD_DIM:(2 * g + 2) * HEAD_DIM] = _dot(xrc.astype(BF16), wax_ref[g])

        tri = tri_ref[...]
        causal = tri > 0.0
        bcol = _cumulative_log_forget(tri, gcol)
        lane = lax.broadcasted_iota(jnp.int32, (1, GATE_LANES), 1)
        grow = jnp.where(lane < HEADS, gcol, bcol).T
        _ffn_gate_up(range(2 * HEADS - 1, N_FF_CHUNKS), xn_ref, gu_ref, wg_ref, wu_ref)
        for h in range(HEADS):
            sl = _col_block(h)
            outm_ref[:, sl] = _mlstm_head(h, q_ref[:, sl], k_ref[:, sl], v_ref[:, sl], xc_ref[:, sl], zm_ref[:, sl],
                                          gcol, bcol, grow, causal, ct_ref, n_ref, m_ref,
                                          lnw_ref[:, sl], skip_ref[:, sl])
            _ffn_act(((0, 1, 2, 3), (4, 5, 6), (7, 8), (9, 10))[h], gu_ref, h_ref)

        downs = []
        for g in range(HEADS):
            sl = _col_block(g)
            downs.append(_dot(h_ref[...], wd_ref[:, sl]))
            xrc = xrc_ref[:, sl]
            r = jax.nn.sigmoid(pre_ref[:, 2 * g * HEAD_DIM:(2 * g + 1) * HEAD_DIM] + ba_ref[:, sl])
            i = jax.nn.sigmoid(pre_ref[:, (2 * g + 1) * HEAD_DIM:(2 * g + 2) * HEAD_DIM] + bx_ref[:, sl])
            log_a = RGLRU_C * r * jax.nn.log_sigmoid(lam_ref[:, sl])
            a = jnp.exp(log_a)
            z = -jnp.tanh(log_a) * (1.0 + a * a)
            u = jnp.where(z > 0.0, z * lax.rsqrt(z), 0.0) * (i * xrc)
            hseq = _linear_scan(a, u, hr_ref, sl)
            outr_ref[:, sl] = hseq * jax.nn.gelu(yr_ref[:, sl])

        o_ref[...] = _time_unpermute(_rms(x2_ref[prev] + 0.5 * jnp.concatenate(downs, axis=1), g2_ref[...]))
        nm = _rms(outm_ref[...], onm_ref[...]).astype(BF16)
        nr = _rms(outr_ref[...], onr_ref[...]).astype(BF16)
        x2_ref[step % 2] = (x1_ref[...] + _dot(nm, wout_ref[0:D_MLSTM, :])
                            + _dot(nr, wout_ref[D_MLSTM:D_MLSTM + D_RGLRU, :]))


def _row_chunks(shape):
    rows, cols = shape
    return pl.BlockSpec((rows // SETUP_STEPS, cols), lambda i: (jnp.minimum(i, SETUP_STEPS - 1), 0))


def _resident(shape):
    nd = len(shape)
    return pl.BlockSpec(shape, lambda *_: (0,) * nd, pipeline_mode=pl.Buffered(1))


def _blockdiag_dense(w):
    rows = w.reshape(HEADS, HEAD_DIM, QKV_BLOCK).astype(F32)
    col = jnp.arange(HEAD_DIM)
    spread = (col[None, :] % QKV_BLOCK == jnp.arange(QKV_BLOCK)[:, None]).astype(F32)
    dense = jnp.einsum('hro,oc->hrc', rows, spread, precision=lax.Precision.HIGHEST)
    same_block = col[:, None] // QKV_BLOCK == col[None, :] // QKV_BLOCK
    return jnp.where(same_block[None], dense, 0.0)


def _row(v):
    return v.reshape(1, -1).astype(F32)


def _permuted_causal():
    rho = jnp.arange(TILE)
    time = (rho % SUBLANES) * SEG + rho // SUBLANES
    return (time[None, :] <= time[:, None]).astype(F32)


def kernel(x, norm_ffn1, ffn1_wg, ffn1_wu, ffn1_wd, norm_mix, w_in, m_conv_w, m_conv_b, m_wq, m_wk, m_wv, m_w_gates, m_b_gates, m_ln_w, m_skip, r_conv_w, r_conv_b, r_w_a, r_b_a, r_w_x, r_b_x, r_lam, out_norm_m, out_norm_r, w_out, norm_ffn2, ffn2_wg, ffn2_wu, ffn2_wd, norm_final):
    bsz, seq, _ = x.shape
    n_tok = bsz * seq
    assert norm_ffn1.shape[0] == 1, "single-layer block"
    assert seq % TILE == 0
    d_proj = w_in.shape[-1]
    n_streams = d_proj // D_MODEL
    n_tiles = n_tok // TILE
    xf = x.reshape(n_tok, D_MODEL)
    params = pltpu.CompilerParams(dimension_semantics=("arbitrary",), vmem_limit_bytes=VMEM_LIMIT)

    assert n_tiles % FFN1_TILES_PER_STEP == 0
    step_rows = FFN1_TILES_PER_STEP * TILE
    x1, proj = pl.pallas_call(
        _ffn_in_kernel,
        grid=(SETUP_STEPS + n_tiles // FFN1_TILES_PER_STEP,),
        in_specs=[
            pl.BlockSpec((step_rows, D_MODEL), lambda i: (jnp.maximum(i - SETUP_STEPS, 0), 0)),
            _resident((1, D_MODEL)),
            _row_chunks((D_MODEL, D_FF)),
            _row_chunks((D_MODEL, D_FF)),
            _row_chunks((D_FF, D_MODEL)),
            _resident((1, D_MODEL)),
            _row_chunks((D_MODEL, d_proj)),
        ],
        out_specs=[pl.BlockSpec((step_rows, D_MODEL), lambda i: (jnp.maximum(i - SETUP_STEPS, 0), 0)),
                   pl.BlockSpec((n_streams, step_rows, D_MODEL),
                                lambda i: (0, jnp.maximum(i - SETUP_STEPS, 0), 0))],
        out_shape=[jax.ShapeDtypeStruct((n_tok, D_MODEL), F32),
                   jax.ShapeDtypeStruct((n_streams, n_tok, D_MODEL), F32)],
        scratch_shapes=[pltpu.VMEM((D_MODEL, D_FF), BF16), pltpu.VMEM((D_MODEL, D_FF), BF16),
                        pltpu.VMEM((D_FF, D_MODEL), BF16), pltpu.VMEM((D_MODEL, d_proj), BF16)],
        compiler_params=params,
        name="ffn1_inproj",
    )(xf, _row(norm_ffn1[0]), ffn1_wg[0], ffn1_wu[0], ffn1_wd[0], _row(norm_mix[0]), w_in[0])

    wqk = jnp.concatenate([_blockdiag_dense(m_wq[0]), _blockdiag_dense(m_wk[0])], axis=-1).astype(BF16)
    wv = _blockdiag_dense(m_wv[0]).astype(BF16)
    wg3 = m_w_gates[0].reshape(3, D_MLSTM, N_GATES)
    wgc = jnp.pad(wg3, ((0, 0), (0, 0), (0, GATE_LANES - N_GATES))).astype(BF16)
    bgc = jnp.pad(m_b_gates[0].astype(F32), (0, GATE_LANES - N_GATES)).reshape(1, GATE_LANES)
    wax = jnp.concatenate([r_w_a[0], r_w_x[0]], axis=-1).astype(BF16)

    last = n_tiles - 1
    tile_idx = lambda i: jnp.clip(i - SETUP_STEPS, 0, last)
    tile_spec = lambda j: pl.BlockSpec((None, TILE, D_MODEL), lambda i, j=j: (j, tile_idx(i), 0))
    weights = [
        _permuted_causal(),
        m_conv_w[0].astype(F32), _row(m_conv_b[0]), wqk, wv, wgc, bgc,
        _row(m_ln_w[0]), _row(m_skip[0]),
        r_conv_w[0].astype(F32), _row(r_conv_b[0]), wax, _row(r_b_a[0]), _row(r_b_x[0]), _row(r_lam[0]),
        _row(out_norm_m[0]), _row(out_norm_r[0]),
        _row(norm_ffn2[0]),
    ]
    streamed = [ffn2_wg[0], ffn2_wu[0], ffn2_wd[0], w_out[0]]
    tile_f32 = pltpu.VMEM((TILE, D_MODEL), F32)
    out = pl.pallas_call(
        functools.partial(_mix_kernel, tiles_per_seq=seq // TILE),
        grid=(SETUP_STEPS + n_tiles + 1,),
        in_specs=[tile_spec(0), tile_spec(1), tile_spec(2), tile_spec(3),
                  pl.BlockSpec((TILE, D_MODEL), lambda i: (tile_idx(i), 0))]
                 + [_resident(w.shape) for w in weights] + [_row_chunks(w.shape) for w in streamed]
                 + [_resident((1, D_MODEL))],
        out_specs=pl.BlockSpec((TILE, D_MODEL), lambda i: (jnp.maximum(i - SETUP_STEPS - 1, 0), 0)),
        out_shape=jax.ShapeDtypeStruct((n_tok, D_MODEL), F32),
        scratch_shapes=[
            pltpu.VMEM((WRAP_ROWS, D_MLSTM), F32),
            pltpu.VMEM((WRAP_ROWS, D_RGLRU), F32),
            pltpu.VMEM((HEADS, HEAD_DIM, HEAD_DIM), F32),
            pltpu.VMEM((HEADS, HEAD_DIM), F32),
            pltpu.VMEM((HEADS, GATE_LANES), F32),
            pltpu.VMEM((SUBLANES, D_RGLRU), F32),
            pltpu.VMEM((2, TILE, D_MODEL), F32),
            pltpu.VMEM((TILE, D_MODEL), BF16),
            pltpu.VMEM((2, TILE, D_FF), F32),
            pltpu.VMEM((TILE, D_FF), BF16),
            tile_f32, tile_f32, tile_f32,
            tile_f32,
            tile_f32,
            pltpu.VMEM((TILE, 2 * D_RGLRU), F32),
            tile_f32, tile_f32,
            pltpu.VMEM((D_MODEL, D_FF), BF16), pltpu.VMEM((D_MODEL, D_FF), BF16),
            pltpu.VMEM((D_FF, D_MODEL), BF16),
            pltpu.VMEM((D_MLSTM + D_RGLRU, D_MODEL), BF16),
            pltpu.VMEM((2, D_MLSTM, GATE_LANES), BF16),
        ],
        compiler_params=params,
        name="mixers_ffn2",
    )(proj, proj, proj, proj, x1, *weights, *streamed, _row(norm_final))
    return out.reshape(bsz, seq, D_MODEL)
```
